```python
import math
import jax, jax.numpy as jnp
from jax import lax
import numpy as np

D_MODEL = 1024
BATCH = 4
SEQ = 4096
DEPTH = 4
DEC_BATCH = 128
DEC_SEQ = 4
PAST_LEN = 8192
PAGE_SIZE = 128

N_META = 16
N_MIXERS = 2
N_MLA_LAYERS = (DEPTH + 1) // 2
N_LRU_LAYERS = DEPTH // 2
N_HEADS = 16
Q_LORA = 384
KV_LORA = 256
QK_NOPE = 64
QK_ROPE = 32
V_HEAD = 64
ROPE_THETA = 10000.0
ATTN_SCALE = (QK_NOPE + QK_ROPE) ** -0.5
Q_BLOCK = 128
LRU_WIDTH = D_MODEL
LRU_BLOCKS = 8
LRU_BLOCK = LRU_WIDTH // LRU_BLOCKS
CONV_W = 4
LRU_C = 8.0
D_FF = ((8 * D_MODEL + 3 * 256 - 1) // (3 * 256)) * 256
ALPHA = (2 * DEPTH) ** 0.25
BETA = (8 * DEPTH) ** -0.25
NORM_EPS = 1e-5

kernel_name = 'hybrid_mla_rglru_decoder_step'


def layer_norm(x, g, b):
    xf = x.astype(jnp.float32)
    mu = jnp.mean(xf, axis=-1, keepdims=True)
    var = jnp.mean(jnp.square(xf - mu), axis=-1, keepdims=True)
    y = (xf - mu) * lax.rsqrt(var + NORM_EPS) * g.astype(jnp.float32) + b.astype(jnp.float32)
    return y.astype(x.dtype)


def rms_norm(x, g):
    xf = x.astype(jnp.float32)
    y = xf * lax.rsqrt(jnp.mean(jnp.square(xf), axis=-1, keepdims=True) + NORM_EPS) * g.astype(jnp.float32)
    return y.astype(x.dtype)


def rope(x, pos):
    half = QK_ROPE // 2
    freqs = ROPE_THETA ** (-jnp.arange(half, dtype=jnp.float32) / half)
    ang = pos.astype(jnp.float32)[:, None] * freqs[None, :]
    cos, sin = jnp.cos(ang), jnp.sin(ang)
    if x.ndim == 4:
        cos, sin = cos[:, None, :], sin[:, None, :]
    xf = x.astype(jnp.float32)
    x1, x2 = xf[..., :half], xf[..., half:]
    out = jnp.concatenate([x1 * cos - x2 * sin, x2 * cos + x1 * sin], axis=-1)
    return out.astype(x.dtype)


def mla_project(x, pos, w_in, q_norm, w_q_up, kv_norm, w_uk):
    B, L, _ = x.shape
    z = x @ w_in
    cq = rms_norm(z[..., :Q_LORA], q_norm)
    ckv = rms_norm(z[..., Q_LORA:Q_LORA + KV_LORA], kv_norm)
    kr = rope(z[..., Q_LORA + KV_LORA:], pos)
    q = (cq @ w_q_up).reshape(B, L, N_HEADS, QK_NOPE + QK_ROPE)
    q_rope = rope(q[..., QK_NOPE:], pos)
    q_lat = jnp.einsum('blhn,chn->blhc', q[..., :QK_NOPE], w_uk)
    return q_lat, q_rope, ckv, kr


def mla_attend(q_lat, q_rope, c_kv, k_rope, q_pos, k_pos):
    s = (jnp.einsum('bqhc,bkc->bhqk', q_lat, c_kv).astype(jnp.float32)
         + jnp.einsum('bqhr,bkr->bhqk', q_rope, k_rope).astype(jnp.float32)) * ATTN_SCALE
    mask = k_pos[None, :] <= q_pos[:, None]
    p = jax.nn.softmax(jnp.where(mask, s, -jnp.inf), axis=-1)
    return jnp.einsum('bhqk,bkc->bqhc', p.astype(c_kv.dtype), c_kv)


def mla_prompt_attention(q_lat, q_rope, c_kv, k_rope):
    B, L = q_lat.shape[:2]
    n_blk = -(-L // Q_BLOCK)
    pad = n_blk * Q_BLOCK - L
    ql = jnp.pad(q_lat, ((0, 0), (0, pad), (0, 0), (0, 0)))
    qr = jnp.pad(q_rope, ((0, 0), (0, pad), (0, 0), (0, 0)))
    ql = ql.reshape(B, n_blk, Q_BLOCK, N_HEADS, KV_LORA).swapaxes(0, 1)
    qr = qr.reshape(B, n_blk, Q_BLOCK, N_HEADS, QK_ROPE).swapaxes(0, 1)
    qpos = jnp.arange(n_blk * Q_BLOCK).reshape(n_blk, Q_BLOCK)
    kpos = jnp.arange(L)
    o = lax.map(lambda a: mla_attend(a[0], a[1], c_kv, k_rope, a[2], kpos), (ql, qr, qpos))
    return o.swapaxes(0, 1).reshape(B, n_blk * Q_BLOCK, N_HEADS, KV_LORA)[:, :L]


def mla_output(o_lat, w_uv, w_o):
    B, L = o_lat.shape[:2]
    v = jnp.einsum('blhc,chv->blhv', o_lat, w_uv).reshape(B, L, N_HEADS * V_HEAD)
    return v @ w_o


def causal_conv(u, buf, w, b):
    L = u.shape[1]
    padded = jnp.concatenate([buf, u], axis=1)
    out = b + sum(padded[:, j:j + L] * w[j] for j in range(CONV_W))
    return out, padded[:, L:]


def rglru(xc, h0, w_a, b_a, w_i, b_i, lam):
    B, L, W = xc.shape
    xf = xc.astype(jnp.float32)
    xb = xf.reshape(B, L, LRU_BLOCKS, LRU_BLOCK)
    r = jax.nn.sigmoid(jnp.einsum('blnk,nkj->blnj', xb, w_a.astype(jnp.float32)).reshape(B, L, W) + b_a.astype(jnp.float32))
    i = jax.nn.sigmoid(jnp.einsum('blnk,nkj->blnj', xb, w_i.astype(jnp.float32)).reshape(B, L, W) + b_i.astype(jnp.float32))
    log_a = -LRU_C * r * jax.nn.softplus(-lam.astype(jnp.float32))
    a = jnp.exp(log_a)
    u = jnp.sqrt(jnp.maximum(-jnp.expm1(2.0 * log_a), 0.0)) * (i * xf)

    def step(h, au):
        h = au[0] * h + au[1]
        return h, h

    hT, hs = lax.scan(step, h0.astype(jnp.float32), (a.swapaxes(0, 1), u.swapaxes(0, 1)))
    return hs.swapaxes(0, 1), hT


def lru_mixer(x, h0, conv_buf, w_in, conv_w, conv_b, w_a, b_a, w_i, b_i, lam, w_o):
    z = x @ w_in
    xr, g = z[..., :LRU_WIDTH], z[..., LRU_WIDTH:]
    xc, new_buf = causal_conv(xr, conv_buf, conv_w, conv_b)
    hs, hT = rglru(xc, h0, w_a, b_a, w_i, b_i, lam)
    y = (hs.astype(x.dtype) * jax.nn.gelu(g)) @ w_o
    return y, hT.astype(x.dtype), new_buf


def swiglu(x, w_gu, w_down):
    z = x @ w_gu
    return (jax.nn.silu(z[..., :D_FF]) * z[..., D_FF:]) @ w_down


def setup_inputs(seed: int = 0) -> dict:
    key = jax.random.key(seed)
    ks = jax.random.split(key, 40)
    n_pages = PAST_LEN // PAGE_SIZE
    n_pool = (DEC_BATCH * n_pages * 5) // 4
    f32 = jnp.float32

    def nrm(k, shape, s):
        return jax.random.normal(k, shape, f32) * s

    perm = jax.random.permutation(ks[0], n_pool)[:DEC_BATCH * n_pages]
    page_table = perm.reshape(DEC_BATCH, n_pages).astype(jnp.int32)
    a_base = jax.random.uniform(ks[1], (N_LRU_LAYERS, LRU_WIDTH), f32, 0.9, 0.999) ** (1.0 / LRU_C)
    lam = jnp.log(a_base) - jnp.log1p(-a_base)
    return {
        'x_prompt': nrm(ks[2], (BATCH, SEQ, D_MODEL), 1.0),
        'x_sample': nrm(ks[3], (DEC_BATCH, DEC_SEQ, D_MODEL), 1.0),
        'cache_latent': nrm(ks[4], (N_MLA_LAYERS, n_pool, PAGE_SIZE, KV_LORA), 1.0),
        'cache_krope': nrm(ks[5], (N_MLA_LAYERS, n_pool, PAGE_SIZE, QK_ROPE), 1.0),
        'page_table': page_table,
        'state_lru_h': nrm(ks[6], (N_LRU_LAYERS, DEC_BATCH, LRU_WIDTH), 0.5),
        'state_conv': nrm(ks[7], (N_LRU_LAYERS, DEC_BATCH, CONV_W - 1, LRU_WIDTH), 1.0),
        'meta_tokens': nrm(ks[8], (N_META, D_MODEL), 1.0),
        'mla_w_in': nrm(ks[9], (N_MLA_LAYERS, D_MODEL, Q_LORA + KV_LORA + QK_ROPE), D_MODEL ** -0.5),
        'mla_q_norm': 1.0 + nrm(ks[10], (N_MLA_LAYERS, Q_LORA), 0.02),
        'mla_w_q_up': nrm(ks[11], (N_MLA_LAYERS, Q_LORA, N_HEADS * (QK_NOPE + QK_ROPE)), Q_LORA ** -0.5),
        'mla_kv_norm': 1.0 + nrm(ks[12], (N_MLA_LAYERS, KV_LORA), 0.02),
        'mla_w_uk': nrm(ks[13], (N_MLA_LAYERS, KV_LORA, N_HEADS, QK_NOPE), KV_LORA ** -0.5),
        'mla_w_uv': nrm(ks[14], (N_MLA_LAYERS, KV_LORA, N_HEADS, V_HEAD), KV_LORA ** -0.5),
        'mla_w_o': nrm(ks[15], (N_MLA_LAYERS, N_HEADS * V_HEAD, D_MODEL), BETA * (N_HEADS * V_HEAD) ** -0.5),
        'lru_w_in': nrm(ks[16], (N_LRU_LAYERS, D_MODEL, 2 * LRU_WIDTH), D_MODEL ** -0.5),
        'lru_conv_w': nrm(ks[17], (N_LRU_LAYERS, CONV_W, LRU_WIDTH), CONV_W ** -0.5),
        'lru_conv_b': nrm(ks[18], (N_LRU_LAYERS, LRU_WIDTH), 0.02),
        'lru_w_a': nrm(ks[19], (N_LRU_LAYERS, LRU_BLOCKS, LRU_BLOCK, LRU_BLOCK), LRU_BLOCK ** -0.5),
        'lru_b_a': nrm(ks[20], (N_LRU_LAYERS, LRU_WIDTH), 0.02),
        'lru_w_i': nrm(ks[21], (N_LRU_LAYERS, LRU_BLOCKS, LRU_BLOCK, LRU_BLOCK), LRU_BLOCK ** -0.5),
        'lru_b_i': nrm(ks[22], (N_LRU_LAYERS, LRU_WIDTH), 0.02),
        'lru_lambda': lam,
        'lru_w_o': nrm(ks[23], (N_LRU_LAYERS, LRU_WIDTH, D_MODEL), BETA * LRU_WIDTH ** -0.5),
        'ffn_w_gu': nrm(ks[24], (DEPTH, D_MODEL, 2 * D_FF), D_MODEL ** -0.5),
        'ffn_w_down': nrm(ks[25], (DEPTH, D_FF, D_MODEL), BETA * D_FF ** -0.5),
        'ln_g': 1.0 + nrm(ks[26], (DEPTH, 2, D_MODEL), 0.02),
        'ln_b': nrm(ks[27], (DEPTH, 2, D_MODEL), 0.02),
    }


def reference(x_prompt, x_sample, cache_latent, cache_krope, page_table, state_lru_h, state_conv, meta_tokens,
              mla_w_in, mla_q_norm, mla_w_q_up, mla_kv_norm, mla_w_uk, mla_w_uv, mla_w_o,
              lru_w_in, lru_conv_w, lru_conv_b, lru_w_a, lru_b_a, lru_w_i, lru_b_i, lru_lambda, lru_w_o,
              ffn_w_gu, ffn_w_down, ln_g, ln_b):
    Bp = x_prompt.shape[0]
    Bs, T = x_sample.shape[:2]
    past_len = page_table.shape[1] * cache_latent.shape[2]

    meta = jnp.broadcast_to(meta_tokens[None].astype(x_prompt.dtype), (Bp, N_META, D_MODEL))
    xp = jnp.concatenate([meta, x_prompt], axis=1)
    xs = x_sample
    Lp = xp.shape[1]
    pos_p = jnp.arange(Lp)
    pos_s = past_len + jnp.arange(T)
    kpos_s = jnp.arange(past_len + T)

    lat_p, kr_p, lat_s, kr_s = [], [], [], []
    hp_l, cp_l, hs_l, cs_l = [], [], [], []

    for i in range(DEPTH):
        if i % N_MIXERS == 0:
            a = i // N_MIXERS
            ql, qr, c, kr = mla_project(xp, pos_p, mla_w_in[a], mla_q_norm[a], mla_w_q_up[a], mla_kv_norm[a], mla_w_uk[a])
            mp = mla_output(mla_prompt_attention(ql, qr, c, kr), mla_w_uv[a], mla_w_o[a])
            lat_p.append(c)
            kr_p.append(kr)
            ql, qr, c, kr = mla_project(xs, pos_s, mla_w_in[a], mla_q_norm[a], mla_w_q_up[a], mla_kv_norm[a], mla_w_uk[a])
            c_past = cache_latent[a][page_table].reshape(Bs, past_len, KV_LORA)
            kr_past = cache_krope[a][page_table].reshape(Bs, past_len, QK_ROPE)
            c_all = jnp.concatenate([c_past.astype(c.dtype), c], axis=1)
            kr_all = jnp.concatenate([kr_past.astype(kr.dtype), kr], axis=1)
            ms = mla_output(mla_attend(ql, qr, c_all, kr_all, pos_s, kpos_s), mla_w_uv[a], mla_w_o[a])
            lat_s.append(c)
            kr_s.append(kr)
        else:
            b = i // N_MIXERS
            lw = (lru_w_in[b], lru_conv_w[b], lru_conv_b[b], lru_w_a[b], lru_b_a[b], lru_w_i[b], lru_b_i[b], lru_lambda[b], lru_w_o[b])
            h0 = jnp.zeros((Bp, LRU_WIDTH), xp.dtype)
            buf0 = jnp.zeros((Bp, CONV_W - 1, LRU_WIDTH), xp.dtype)
            mp, hT, bufT = lru_mixer(xp, h0, buf0, *lw)
            hp_l.append(hT)
            cp_l.append(bufT)
            ms, hT, bufT = lru_mixer(xs, state_lru_h[b].astype(xs.dtype), state_conv[b].astype(xs.dtype), *lw)
            hs_l.append(hT)
            cs_l.append(bufT)
        xp = layer_norm(ALPHA * xp + mp, ln_g[i, 0], ln_b[i, 0])
        xs = layer_norm(ALPHA * xs + ms, ln_g[i, 0], ln_b[i, 0])
        xp = layer_norm(ALPHA * xp + swiglu(xp, ffn_w_gu[i], ffn_w_down[i]), ln_g[i, 1], ln_b[i, 1])
        xs = layer_norm(ALPHA * xs + swiglu(xs, ffn_w_gu[i], ffn_w_down[i]), ln_g[i, 1], ln_b[i, 1])

    y_prompt = xp[:, N_META:]
    y_sample = xs
    return (y_prompt, y_sample,
            jnp.stack(lat_p), jnp.stack(kr_p), jnp.stack(lat_s), jnp.stack(kr_s),
            jnp.stack(hp_l), jnp.stack(cp_l), jnp.stack(hs_l), jnp.stack(cs_l))
```

```python
import functools

import jax
import jax.numpy as jnp
from jax import lax
from jax.experimental import pallas as pl
from jax.experimental.pallas import tpu as pltpu

N_META = 16
N_HEADS = 16
Q_LORA = 384
KV_LORA = 256
QK_NOPE = 64
QK_ROPE = 32
V_HEAD = 64
ROPE_THETA = 10000.0
ATTN_SCALE = (QK_NOPE + QK_ROPE) ** -0.5
LRU_BLOCKS = 8
CONV_W = 4
LRU_C = 8.0
NORM_EPS = 1e-5

LANES = 128
SUBLANES = 8
HEAD_PAD = LANES
ROW_TILE = 1024
PROJ_TILE = 512
ATTN_BLOCK = 512
SCAN_CHUNK = 512
KV_CHUNK = 1024
VMEM_LIMIT = 56 * 1024 * 1024

F32 = jnp.float32
BF16 = jnp.bfloat16


def _cparams(*sem):
    return pltpu.CompilerParams(dimension_semantics=sem, vmem_limit_bytes=VMEM_LIMIT)


def _dot(a, b):
    return jnp.dot(a, b, preferred_element_type=F32)


def _dot_nt(a, b):
    return lax.dot_general(a, b, (((1,), (1,)), ((), ())), preferred_element_type=F32)


def _full(shape):
    nd = len(shape)
    return pl.BlockSpec(shape, lambda *_: (0,) * nd)


def _layer_norm(y, g, b):
    mu = jnp.mean(y, axis=-1, keepdims=True)
    d = y - mu
    var = jnp.mean(d * d, axis=-1, keepdims=True)
    return d * lax.rsqrt(var + NORM_EPS) * g + b


def _rms_norm(y, g):
    return y * lax.rsqrt(jnp.mean(y * y, axis=-1, keepdims=True) + NORM_EPS) * g


def _mla_proj_kernel(x_ref, w_in_ref, qn_ref, kvn_ref, cos_ref, sin_ref, t1_ref, wq_ref, wuk_ref, wuv_ref,
                     ckv_ref, krkr_ref, cq_ref, q_ref, k_ref, v_ref):
    x = x_ref[...].astype(BF16)
    z = _dot(x, w_in_ref[...])
    cq = _rms_norm(z[:, :Q_LORA], qn_ref[...])
    ckv = _rms_norm(z[:, Q_LORA:Q_LORA + KV_LORA], kvn_ref[...])
    o = Q_LORA + KV_LORA
    krkr = z[:, o:o + LANES] * cos_ref[...] + z[:, o + LANES:o + 2 * LANES] * sin_ref[...]
    ckv_ref[...] = ckv
    krkr_ref[...] = krkr
    cq_b = cq.astype(BF16)
    cq_ref[...] = cq_b
    ckv_b = ckv.astype(BF16)
    t1 = t1_ref[...]
    qz = _dot(cq_b, wq_ref[...])
    for h in range(N_HEADS):
        q_ref[h] = (qz[:, h * HEAD_PAD:(h + 1) * HEAD_PAD] * t1).astype(BF16)
    kz = _dot(ckv_b, wuk_ref[...])
    for h in range(N_HEADS):
        k_ref[h] = (kz[:, h * HEAD_PAD:(h + 1) * HEAD_PAD] + krkr).astype(BF16)
    vz = _dot(ckv_b, wuv_ref[...])
    for h in range(N_HEADS):
        v_ref[h] = vz[:, h * HEAD_PAD:(h + 1) * HEAD_PAD].astype(BF16)


def _mla_proj(x, w_in, qn, kvn, cos4, sin4, t1, wq, wuk, wuv):
    nt, d = x.shape
    t = PROJ_TILE
    hp = N_HEADS * HEAD_PAD
    row = lambda w: pl.BlockSpec((t, w), lambda i: (i, 0))
    head = pl.BlockSpec((N_HEADS, t, HEAD_PAD), lambda i: (0, i, 0))
    return pl.pallas_call(
        _mla_proj_kernel,
        grid=(nt // t,),
        in_specs=[row(d), _full(w_in.shape), _full(qn.shape), _full(kvn.shape), row(LANES), row(LANES), row(LANES),
                  _full(wq.shape), _full(wuk.shape), _full(wuv.shape)],
        out_specs=[row(KV_LORA), row(LANES), row(Q_LORA), head, head, head],
        out_shape=[jax.ShapeDtypeStruct((nt, KV_LORA), F32), jax.ShapeDtypeStruct((nt, LANES), F32),
                   jax.ShapeDtypeStruct((nt, Q_LORA), BF16)]
                  + [jax.ShapeDtypeStruct((N_HEADS, nt, HEAD_PAD), BF16)] * 3,
        compiler_params=_cparams("parallel"),
        name="mla_proj",
    )(x, w_in, qn, kvn, cos4, sin4, t1, wq, wuk, wuv)


def _softmax_block(s, m, l, acc, v):
    m_new = jnp.maximum(m, jnp.max(s, axis=-1, keepdims=True))
    alpha = jnp.exp(m - m_new)
    p = jnp.exp(s - m_new)
    l = alpha * l + jnp.sum(p, axis=-1, keepdims=True)
    acc = alpha * acc + _dot(p.astype(BF16), v)
    return m_new, l, acc


def _attn_prompt_kernel(q_ref, k_ref, v_ref, qm_ref, km_ref, vm_ref, o_ref, om_ref):
    qi = pl.program_id(2)
    blk = ATTN_BLOCK
    n_pair = q_ref.shape[0]

    def one_head(g):
        q = q_ref[g]
        s = _dot_nt(q, km_ref[g])
        m = jnp.max(s, axis=-1, keepdims=True)
        p = jnp.exp(s - m)
        l = jnp.sum(p, axis=-1, keepdims=True)
        acc = _dot(p.astype(BF16), vm_ref[g])

        def body(j, carry):
            off = pl.multiple_of(j * blk, blk)
            s = _dot_nt(q, k_ref[g, pl.ds(off, blk), :])
            return _softmax_block(s, *carry, v_ref[g, pl.ds(off, blk), :])

        m, l, acc = lax.fori_loop(0, qi, body, (m, l, acc))
        off = pl.multiple_of(qi * blk, blk)
        s = _dot_nt(q, k_ref[g, pl.ds(off, blk), :])
        row = lax.broadcasted_iota(jnp.int32, s.shape, 0)
        col = lax.broadcasted_iota(jnp.int32, s.shape, 1)
        s = jnp.where(col <= row, s, -jnp.inf)
        m, l, acc = _softmax_block(s, m, l, acc, v_ref[g, pl.ds(off, blk), :])
        return acc / l

    out = one_head(0)
    for g in range(1, n_pair):
        out = out + one_head(g)
    o_ref[...] = out.astype(o_ref.dtype)

    @pl.when(qi == 0)
    def _():
        tot = None
        for g in range(n_pair):
            s = _dot_nt(qm_ref[g], km_ref[g])
            row = lax.broadcasted_iota(jnp.int32, s.shape, 0)
            col = lax.broadcasted_iota(jnp.int32, s.shape, 1)
            s = jnp.where(col <= row, s, -jnp.inf)
            m = jnp.max(s, axis=-1, keepdims=True)
            p = jnp.exp(s - m)
            o = _dot(p.astype(BF16), vm_ref[g]) / jnp.sum(p, axis=-1, keepdims=True)
            tot = o if tot is None else tot + o
        om_ref[...] = tot.astype(om_ref.dtype)


def _attn_prompt(q, k, v, n_batch, seq):
    n_real = n_batch * seq
    blk = ATTN_BLOCK
    n_q = seq // blk
    pair = 2
    meta_blk0 = n_real // N_META
    qspec = pl.BlockSpec((pair, blk, HEAD_PAD), lambda b, hp, qi: (hp, b * n_q + qi, 0))
    kvspec = pl.BlockSpec((pair, seq, HEAD_PAD), lambda b, hp, qi: (hp, b, 0))
    mspec = pl.BlockSpec((pair, N_META, HEAD_PAD), lambda b, hp, qi: (hp, meta_blk0 + b, 0))
    return pl.pallas_call(
        _attn_prompt_kernel,
        grid=(n_batch, N_HEADS // pair, n_q),
        in_specs=[qspec, kvspec, kvspec, mspec, mspec, mspec],
        out_specs=[pl.BlockSpec((blk, pair * V_HEAD), lambda b, hp, qi: (b * n_q + qi, hp)),
                   pl.BlockSpec((N_META, pair * V_HEAD), lambda b, hp, qi: (b, hp))],
        out_shape=[jax.ShapeDtypeStruct((n_real, N_HEADS * V_HEAD), BF16),
                   jax.ShapeDtypeStruct((n_batch * N_META, N_HEADS * V_HEAD), BF16)],
        compiler_params=_cparams("parallel", "parallel", "arbitrary"),
        name="attn_prompt",
    )(q, k, v, q, k, v)


def _sample_q_kernel(cq_ref, wq_ref, wukt_ref, t1_ref, qlat_ref, qrope_ref):
    qz = _dot(cq_ref[...], wq_ref[...])
    t1 = t1_ref[...]
    for h in range(N_HEADS):
        e = qz[:, h * HEAD_PAD:(h + 1) * HEAD_PAD] * t1
        qlat_ref[h] = _dot(e[:, :QK_NOPE].astype(BF16), wukt_ref[h]).astype(BF16)
        qrope_ref[h] = (e[:, QK_NOPE:QK_NOPE + QK_ROPE] + e[:, QK_NOPE + QK_ROPE:]).astype(BF16)


def _sample_q(cq, wq, wukt, t1):
    n = cq.shape[0]
    return pl.pallas_call(
        _sample_q_kernel,
        grid=(1,),
        in_specs=[_full(cq.shape), _full(wq.shape), _full(wukt.shape), _full(t1.shape)],
        out_specs=[_full((N_HEADS, n, KV_LORA)), _full((N_HEADS, n, QK_ROPE))],
        out_shape=[jax.ShapeDtypeStruct((N_HEADS, n, KV_LORA), BF16),
                   jax.ShapeDtypeStruct((N_HEADS, n, QK_ROPE), BF16)],
        compiler_params=_cparams("arbitrary"),
        name="sample_q",
    )(cq, wq, wukt, t1)


def _attn_sample_kernel(layer, n_pages, page, pt_ref, qlat_ref, qrope_ref, cnew_ref, rnew_ref, clat_hbm, ckr_hbm,
                        o_ref, cbuf, rbuf, sem):
    b = pl.program_id(0)
    nb = pl.num_programs(0)
    slot = b % 2

    def copies(seq, slot_, p):
        phys = pt_ref[seq, p]
        rows = pl.ds(pl.multiple_of(p * page, page), page)
        return (pltpu.make_async_copy(clat_hbm.at[layer, phys], cbuf.at[slot_, rows, :], sem.at[0, slot_]),
                pltpu.make_async_copy(ckr_hbm.at[layer, phys], rbuf.at[slot_, rows, :], sem.at[1, slot_]))

    def fetch(seq, slot_):
        def body(p, c):
            for cp in copies(seq, slot_, p):
                cp.start()
            return c
        lax.fori_loop(0, n_pages, body, 0)

    @pl.when(b == 0)
    def _():
        fetch(0, 0)

    @pl.when(b + 1 < nb)
    def _():
        fetch(b + 1, 1 - slot)

    def wait_body(p, c):
        for cp in copies(b, slot, p):
            cp.wait()
        return c
    lax.fori_loop(0, n_pages, wait_body, 0)

    qlat = qlat_ref[0]
    qrope = qrope_ref[0]
    s = _dot_nt(qlat, cnew_ref[0].astype(BF16)) + _dot_nt(qrope, rnew_ref[0].astype(BF16))
    row = lax.broadcasted_iota(jnp.int32, s.shape, 0)
    col = lax.broadcasted_iota(jnp.int32, s.shape, 1)
    s = jnp.where(col * N_HEADS <= row, s, -jnp.inf)
    m = jnp.max(s, axis=-1, keepdims=True)
    p = jnp.exp(s - m)
    l = jnp.sum(p, axis=-1, keepdims=True)
    acc = _dot(p.astype(BF16), cnew_ref[0].astype(BF16))
    for ck in range(n_pages * page // KV_CHUNK):
        rows = pl.ds(ck * KV_CHUNK, KV_CHUNK)
        c = cbuf[slot, rows, :].astype(BF16)
        r = rbuf[slot, rows, :].astype(BF16)
        s = _dot_nt(qlat, c) + _dot_nt(qrope, r)
        m, l, acc = _softmax_block(s, m, l, acc, c)
    o_ref[0] = acc / l


def _attn_sample(layer, page_table, qlat, qrope, cnew, rnew, cache_latent, cache_krope):
    n_seq, n_pages = page_table.shape
    page = cache_latent.shape[2]
    rows = qlat.shape[1]
    past = n_pages * page
    blk = lambda a: pl.BlockSpec((1,) + a.shape[1:], lambda b, pt: (b, 0, 0))
    return pl.pallas_call(
        functools.partial(_attn_sample_kernel, layer, n_pages, page),
        grid_spec=pltpu.PrefetchScalarGridSpec(
            num_scalar_prefetch=1,
            grid=(n_seq,),
            in_specs=[blk(qlat), blk(qrope), blk(cnew), blk(rnew),
                      pl.BlockSpec(memory_space=pl.ANY), pl.BlockSpec(memory_space=pl.ANY)],
            out_specs=pl.BlockSpec((1, rows, KV_LORA), lambda b, pt: (b, 0, 0)),
            scratch_shapes=[pltpu.VMEM((2, past, KV_LORA), F32), pltpu.VMEM((2, past, QK_ROPE), F32),
                            pltpu.SemaphoreType.DMA((2, 2))],
        ),
        out_shape=jax.ShapeDtypeStruct((n_seq, rows, KV_LORA), F32),
        compiler_params=_cparams("arbitrary"),
        name="attn_sample",
    )(page_table, qlat, qrope, cnew, rnew, cache_latent, cache_krope)


def _sample_ov_kernel(o_ref, wuv_ref, out_ref):
    for hp in range(N_HEADS // 2):
        a = _dot(o_ref[2 * hp], wuv_ref[2 * hp]) + _dot(o_ref[2 * hp + 1], wuv_ref[2 * hp + 1])
        out_ref[:, hp * LANES:(hp + 1) * LANES] = a.astype(out_ref.dtype)


def _sample_ov(o, wuv):
    n = o.shape[1]
    return pl.pallas_call(
        _sample_ov_kernel,
        grid=(1,),
        in_specs=[_full(o.shape), _full(wuv.shape)],
        out_specs=_full((n, N_HEADS * V_HEAD)),
        out_shape=jax.ShapeDtypeStruct((n, N_HEADS * V_HEAD), BF16),
        compiler_params=_cparams("arbitrary"),
        name="sample_ov",
    )(o, wuv)


def _proj_ln_kernel(alpha, x_ref, y_ref, w_ref, g_ref, b_ref, o_ref):
    y = alpha * x_ref[...] + _dot(y_ref[...], w_ref[...])
    o_ref[...] = _layer_norm(y, g_ref[...], b_ref[...])


def _proj_ln(alpha, x, y, w, g, b):
    nt, d = x.shape
    t = ROW_TILE
    row = pl.BlockSpec((t, d), lambda i: (i, 0))
    return pl.pallas_call(
        functools.partial(_proj_ln_kernel, alpha),
        grid=(nt // t,),
        in_specs=[row, pl.BlockSpec((t, y.shape[1]), lambda i: (i, 0)), _full(w.shape), _full(g.shape), _full(b.shape)],
        out_specs=row,
        out_shape=jax.ShapeDtypeStruct((nt, d), F32),
        compiler_params=_cparams("parallel"),
        name="proj_ln",
    )(x, y, w, g, b)


def _ffn_kernel(alpha, x_ref, wgu_ref, wd_ref, g_ref, b_ref, o_ref, acc_ref):
    x = x_ref[...]
    xb = x.astype(BF16)
    n_chunk, _, two_c = wgu_ref.shape
    c = two_c // 2
    for j in range(n_chunk):
        z = _dot(xb, wgu_ref[j])
        gate, up = z[:, :c], z[:, c:]
        h = (gate * jax.nn.sigmoid(gate) * up).astype(BF16)
        part = _dot(h, wd_ref[j])
        if j == 0:
            acc_ref[...] = part
        else:
            acc_ref[...] += part
    o_ref[...] = _layer_norm(alpha * x + acc_ref[...], g_ref[...], b_ref[...])


def _ffn(alpha, x, wgu, wd, g, b):
    nt, d = x.shape
    t = ROW_TILE
    row = pl.BlockSpec((t, d), lambda i: (i, 0))
    once = lambda a: pl.BlockSpec(a.shape, lambda i: (0,) * a.ndim, pipeline_mode=pl.Buffered(1))
    return pl.pallas_call(
        functools.partial(_ffn_kernel, alpha),
        grid=(nt // t,),
        in_specs=[row, once(wgu), once(wd), _full(g.shape), _full(b.shape)],
        out_specs=row,
        out_shape=jax.ShapeDtypeStruct((nt, d), F32),
        scratch_shapes=[pltpu.VMEM((t, d), F32)],
        compiler_params=_cparams("parallel"),
        name="ffn",
    )(x, wgu, wd, g, b)


def _gelu_tanh(x):
    return 0.5 * x * (1.0 + jnp.tanh(0.7978845608028654 * (x + 0.044715 * (x * x * x))))


def _lru_in_kernel(x_ref, w_ref, xr_ref, gg_ref):
    z = _dot(x_ref[...].astype(BF16), w_ref[...])
    w = xr_ref.shape[1]
    xr_ref[...] = z[:, :w]
    gg_ref[...] = _gelu_tanh(z[:, w:]).astype(gg_ref.dtype)


def _lru_in(x, w):
    nt, d = x.shape
    t = ROW_TILE
    wid = w.shape[1] // 2
    row = lambda n: pl.BlockSpec((t, n), lambda i: (i, 0))
    return pl.pallas_call(
        _lru_in_kernel,
        grid=(nt // t,),
        in_specs=[row(d), _full(w.shape)],
        out_specs=[row(wid), row(wid)],
        out_shape=[jax.ShapeDtypeStruct((nt, wid), F32), jax.ShapeDtypeStruct((nt, wid), BF16)],
        compiler_params=_cparams("parallel"),
        name="lru_in",
    )(x, w)


def _lru_gates(xc, wai, ba, bi, sp):
    zg = _dot(xc.astype(BF16), wai)
    r = jax.nn.sigmoid(zg[:, :LANES] + ba)
    i = jax.nn.sigmoid(zg[:, LANES:] + bi)
    log_a = (-LRU_C) * r * sp
    a = jnp.exp(log_a)
    th = jnp.tanh(log_a)
    one_minus_a2 = -2.0 * th / (1.0 - th)
    u = jnp.sqrt(jnp.maximum(one_minus_a2, 0.0)) * (i * xc)
    return a, u


def _softplus(y):
    return jnp.maximum(y, 0.0) + jnp.log1p(jnp.exp(-jnp.abs(y)))


def _shift_rows(v, d, fill):
    row = lax.broadcasted_iota(jnp.int32, v.shape, 0)
    return jnp.where(row >= d, pltpu.roll(v, d, axis=0), fill)


def _scan_block(xr, gg, tail, h_in, cw, cb, wai, ba, bi, lam):
    n = xr.shape[0]
    xp = jnp.concatenate([tail, xr], axis=0)
    xc = cb + cw[CONV_W - 1:CONV_W] * xr
    for j in range(1, CONV_W):
        xc = xc + cw[CONV_W - 1 - j:CONV_W - j] * pltpu.roll(xp, j, axis=0)[SUBLANES:]
    a, u = _lru_gates(xc, wai, ba, bi, _softplus(-lam))
    d = 1
    while d < n:
        u = u + a * _shift_rows(u, d, 0.0)
        a = a * _shift_rows(a, d, 1.0)
        d *= 2
    hs = a * h_in + u
    return (hs * gg.astype(F32)), hs[n - 1:n], xr[n - SUBLANES:]


def _lru_scan_prompt_kernel(xr_ref, gg_ref, xrm_ref, ggm_ref, cw_ref, cb_ref, wai_ref, ba_ref, bi_ref, lam_ref,
                            y_ref, ym_ref, hT_ref, h_scr, tail_scr):
    c = pl.program_id(1)

    def run(xr_r, gg_r, y_r):
        for n in range(LRU_BLOCKS):
            ln = slice(n * LANES, (n + 1) * LANES)
            y, h, tail = _scan_block(xr_r[:, ln], gg_r[:, ln], tail_scr[:, ln], h_scr[0:1, ln], cw_ref[:, ln],
                                     cb_ref[:, ln], wai_ref[n], ba_ref[:, ln], bi_ref[:, ln], lam_ref[:, ln])
            y_r[:, ln] = y.astype(y_r.dtype)
            h_scr[:, ln] = jnp.broadcast_to(h, (SUBLANES, LANES))
            tail_scr[:, ln] = tail

    @pl.when(c == 0)
    def _():
        h_scr[...] = jnp.zeros_like(h_scr)
        tail_scr[...] = jnp.zeros_like(tail_scr)
        run(xrm_ref, ggm_ref, ym_ref)

    run(xr_ref, gg_ref, y_ref)
    hT_ref[...] = h_scr[...]


def _lru_scan_prompt(xr, gg, n_batch, seq, cw, cb, wai, ba, bi, lam):
    n_real = n_batch * seq
    w = xr.shape[1]
    t = SCAN_CHUNK
    n_c = seq // t
    meta_blk0 = n_real // N_META
    real = pl.BlockSpec((t, w), lambda b, c: (b * n_c + c, 0))
    meta = pl.BlockSpec((N_META, w), lambda b, c: (meta_blk0 + b, 0))
    return pl.pallas_call(
        _lru_scan_prompt_kernel,
        grid=(n_batch, n_c),
        in_specs=[real, real, meta, meta, _full(cw.shape), _full(cb.shape), _full(wai.shape), _full(ba.shape),
                  _full(bi.shape), _full(lam.shape)],
        out_specs=[real, pl.BlockSpec((N_META, w), lambda b, c: (b, 0)), pl.BlockSpec((SUBLANES, w), lambda b, c: (b, 0))],
        out_shape=[jax.ShapeDtypeStruct((n_real, w), BF16), jax.ShapeDtypeStruct((n_batch * N_META, w), BF16),
                   jax.ShapeDtypeStruct((n_batch * SUBLANES, w), F32)],
        scratch_shapes=[pltpu.VMEM((SUBLANES, w), F32), pltpu.VMEM((SUBLANES, w), F32)],
        compiler_params=_cparams("parallel", "arbitrary"),
        name="lru_scan_prompt",
    )(xr, gg, xr, gg, cw, cb, wai, ba, bi, lam)


def _lru_scan_sample_kernel(xr_ref, gg_ref, buf_ref, h0_ref, cw_ref, cb_ref, wai_ref, ba_ref, bi_ref, lam_ref,
                            y_ref, hT_ref):
    n_t = xr_ref.shape[0]
    for n in range(LRU_BLOCKS):
        ln = slice(n * LANES, (n + 1) * LANES)
        rows = [buf_ref[j, :, ln] for j in range(CONV_W - 1)] + [xr_ref[t, :, ln] for t in range(n_t)]
        sp = _softplus(-lam_ref[:, ln])
        h = h0_ref[:, ln]
        for t in range(n_t):
            xc = cb_ref[:, ln]
            for j in range(CONV_W):
                xc = xc + cw_ref[j:j + 1, ln] * rows[t + j]
            a, u = _lru_gates(xc, wai_ref[n], ba_ref[:, ln], bi_ref[:, ln], sp)
            h = a * h + u
            y_ref[t, :, ln] = (h * gg_ref[t, :, ln].astype(F32)).astype(y_ref.dtype)
        hT_ref[:, ln] = h


def _lru_scan_sample(xr, gg, buf, h0, cw, cb, wai, ba, bi, lam):
    args = (xr, gg, buf, h0, cw, cb, wai, ba, bi, lam)
    return pl.pallas_call(
        _lru_scan_sample_kernel,
        grid=(1,),
        in_specs=[_full(a.shape) for a in args],
        out_specs=[_full(xr.shape), _full(h0.shape)],
        out_shape=[jax.ShapeDtypeStruct(xr.shape, BF16), jax.ShapeDtypeStruct(h0.shape, F32)],
        compiler_params=_cparams("arbitrary"),
        name="lru_scan_sample",
    )(*args)


def _rot_cols(w):
    half = w.shape[-1] // 2
    return jnp.concatenate([-w[..., half:], w[..., :half]], axis=-1)


def kernel(x_prompt, x_sample, cache_latent, cache_krope, page_table, state_lru_h, state_conv, meta_tokens,
           mla_w_in, mla_q_norm, mla_w_q_up, mla_kv_norm, mla_w_uk, mla_w_uv, mla_w_o,
           lru_w_in, lru_conv_w, lru_conv_b, lru_w_a, lru_b_a, lru_w_i, lru_b_i, lru_lambda, lru_w_o,
           ffn_w_gu, ffn_w_down, ln_g, ln_b):
    n_b, seq, d = x_prompt.shape
    n_s, n_t = x_sample.shape[:2]
    depth = ffn_w_gu.shape[0]
    d_ff = ffn_w_down.shape[1]
    alpha = (2 * depth) ** 0.25
    past_len = page_table.shape[1] * cache_latent.shape[2]
    n_real = n_b * seq
    n_meta = n_b * N_META
    n_samp = n_s * n_t
    assert seq % ATTN_BLOCK == 0 and seq % SCAN_CHUNK == 0 and n_real % ROW_TILE == 0
    assert n_meta + n_samp <= ROW_TILE and d_ff % 256 == 0
    n_pad = ROW_TILE - n_meta - n_samp
    nt = n_real + ROW_TILE
    s0, s1 = n_real + n_meta, n_real + n_meta + n_samp

    x = jnp.concatenate([x_prompt.reshape(n_real, d),
                         jnp.broadcast_to(meta_tokens[None], (n_b, N_META, d)).reshape(n_meta, d),
                         x_sample.transpose(1, 0, 2).reshape(n_samp, d),
                         jnp.zeros((n_pad, d), F32)], axis=0)

    pos = jnp.concatenate([jnp.tile(N_META + jnp.arange(seq), n_b), jnp.tile(jnp.arange(N_META), n_b),
                           jnp.repeat(past_len + jnp.arange(n_t), n_s), jnp.zeros((n_pad,), jnp.int32)])
    half = QK_ROPE // 2
    freqs = ROPE_THETA ** (-jnp.arange(half, dtype=F32) / half)
    ang = pos.astype(F32)[:, None] * freqs[None, :]
    cos2 = jnp.tile(jnp.cos(ang), (1, 2))
    sin2 = jnp.tile(jnp.sin(ang), (1, 2))
    zeros64 = jnp.zeros((nt, QK_NOPE), F32)
    cos4 = jnp.concatenate([zeros64, cos2, cos2], axis=1)
    sin4 = jnp.concatenate([zeros64, sin2, sin2], axis=1)
    t1 = jnp.concatenate([jnp.full((nt, QK_NOPE), ATTN_SCALE, F32), cos2 * ATTN_SCALE, sin2 * ATTN_SCALE], axis=1)

    even = (jnp.arange(N_HEADS) % 2 == 0)[None, :, None]

    lat_p, kr_p, lat_s, kr_s, hp_l, cp_l, hs_l, cs_l = [], [], [], [], [], [], [], []
    for i in range(depth):
        g0, b0 = ln_g[i, 0][None], ln_b[i, 0][None]
        g1, b1 = ln_g[i, 1][None], ln_b[i, 1][None]
        if i % 2 == 0:
            a = i // 2
            w_in = mla_w_in[a]
            wr = w_in[:, Q_LORA + KV_LORA:]
            z64 = jnp.zeros((d, QK_NOPE), F32)
            w_in_ext = jnp.concatenate([w_in[:, :Q_LORA + KV_LORA], z64, wr, wr, z64, _rot_cols(wr), _rot_cols(wr)],
                                       axis=1).astype(BF16)
            wq3 = mla_w_q_up[a].reshape(Q_LORA, N_HEADS, QK_NOPE + QK_ROPE)
            wq_ext = jnp.concatenate([wq3, _rot_cols(wq3[..., QK_NOPE:])], axis=-1).reshape(Q_LORA, -1).astype(BF16)
            wuk = mla_w_uk[a]
            wuk_pad = jnp.concatenate([wuk, jnp.zeros_like(wuk)], axis=-1).reshape(KV_LORA, -1).astype(BF16)
            wuv = mla_w_uv[a]
            zv = jnp.zeros_like(wuv)
            wuv_pad3 = jnp.where(even, jnp.concatenate([wuv, zv], -1), jnp.concatenate([zv, wuv], -1)).astype(BF16)
            wuv_pad = wuv_pad3.reshape(KV_LORA, -1)

            ckv, krkr, cq, q, k, v = _mla_proj(x, w_in_ext, mla_q_norm[a][None], mla_kv_norm[a][None],
                                               cos4, sin4, t1, wq_ext, wuk_pad, wuv_pad)
            kr = krkr[:, QK_NOPE:QK_NOPE + QK_ROPE]
            o_real, o_meta = _attn_prompt(q, k, v, n_b, seq)

            wukt = wuk.transpose(1, 2, 0).astype(BF16)
            qlat, qrope = _sample_q(cq[s0:s1], wq_ext, wukt, t1[s0:s1])
            per_seq = lambda t: t.reshape(N_HEADS, n_t, n_s, -1).transpose(2, 1, 0, 3).reshape(n_s, n_t * N_HEADS, -1)
            new_rows = lambda t: jnp.pad(t.reshape(n_t, n_s, -1).transpose(1, 0, 2),
                                         ((0, 0), (0, SUBLANES - n_t), (0, 0)))
            o_lat = _attn_sample(a, page_table, per_seq(qlat), per_seq(qrope), new_rows(ckv[s0:s1]),
                                 new_rows(kr[s0:s1]), cache_latent, cache_krope)
            o_lat = o_lat.reshape(n_s, n_t, N_HEADS, KV_LORA).transpose(2, 1, 0, 3).reshape(N_HEADS, n_samp, KV_LORA)
            o_samp = _sample_ov(o_lat.astype(BF16), wuv_pad3.transpose(1, 0, 2))

            mix_in = jnp.concatenate([o_real, o_meta, o_samp, jnp.zeros((n_pad, d), BF16)], axis=0)
            w_o = mla_w_o[a].astype(BF16)

            tok = lambda t: jnp.concatenate([t[n_real:s0].reshape(n_b, N_META, -1), t[:n_real].reshape(n_b, seq, -1)], 1)
            samp = lambda t: t[s0:s1].reshape(n_t, n_s, -1).transpose(1, 0, 2)
            lat_p.append(tok(ckv))
            kr_p.append(tok(kr))
            lat_s.append(samp(ckv))
            kr_s.append(samp(kr))
        else:
            lb = i // 2
            xr, gg = _lru_in(x, lru_w_in[lb].astype(BF16))
            wai = jnp.concatenate([lru_w_a[lb], lru_w_i[lb]], axis=-1).astype(BF16)
            lw = (lru_conv_w[lb], lru_conv_b[lb][None], wai, lru_b_a[lb][None], lru_b_i[lb][None],
                  lru_lambda[lb][None])
            y_real, y_meta, h_p = _lru_scan_prompt(xr, gg, n_b, seq, *lw)
            xr_s = xr[s0:s1].reshape(n_t, n_s, -1)
            y_samp, h_s = _lru_scan_sample(xr_s, gg[s0:s1].reshape(n_t, n_s, -1), state_conv[lb].transpose(1, 0, 2),
                                           state_lru_h[lb], *lw)
            mix_in = jnp.concatenate([y_real, y_meta, y_samp.reshape(n_samp, -1), jnp.zeros((n_pad, d), BF16)], axis=0)
            w_o = lru_w_o[lb].astype(BF16)

            hp_l.append(h_p.reshape(n_b, SUBLANES, -1)[:, 0])
            cp_l.append(xr[:n_real].reshape(n_b, seq, -1)[:, seq - (CONV_W - 1):])
            hs_l.append(h_s)
            cs_l.append(jnp.concatenate([state_conv[lb].transpose(1, 0, 2), xr_s], axis=0)[-(CONV_W - 1):]
                        .transpose(1, 0, 2))

        x = _proj_ln(alpha, x, mix_in, w_o, g0, b0)
        wg, wu = ffn_w_gu[i][:, :d_ff], ffn_w_gu[i][:, d_ff:]
        n_chunk = d_ff // 256
        wgu = jnp.concatenate([wg.reshape(d, n_chunk, 256), wu.reshape(d, n_chunk, 256)], axis=-1)
        wgu = wgu.transpose(1, 0, 2).astype(BF16)
        wd = ffn_w_down[i].reshape(n_chunk, 256, d).astype(BF16)
        x = _ffn(alpha, x, wgu, wd, g1, b1)

    y_prompt = x[:n_real].reshape(n_b, seq, d)
    y_sample = x[s0:s1].reshape(n_t, n_s, d).transpose(1, 0, 2)
    return (y_prompt, y_sample, jnp.stack(lat_p), jnp.stack(kr_p), jnp.stack(lat_s), jnp.stack(kr_s),
            jnp.stack(hp_l), jnp.stack(cp_l), jnp.stack(hs_l), jnp.stack(cs_l))
```

```python
import functools

import jax
import jax.numpy as jnp
from jax import lax
from jax.experimental import pallas as pl
from jax.experimental.pallas import tpu as pltpu

N_META = 16
N_HEADS = 16
Q_LORA = 384
KV_LORA = 256
QK_NOPE = 64
QK_ROPE = 32
V_HEAD = 64
ROPE_THETA = 10000.0
ATTN_SCALE = (QK_NOPE + QK_ROPE) ** -0.5
LRU_BLOCKS = 8
CONV_W = 4
LRU_C = 8.0
NORM_EPS = 1e-5

LANES = 128
SUBLANES = 8
HEAD_PAD = LANES
ROW_TILE = 1024
PROJ_TILE = 512
ATTN_BLOCK = 512
HEADS_PER_STEP = 4
SCAN_CHUNK = 512
KV_CHUNK = 1024
FFN_CHUNK = 512
VMEM_LIMIT = 56 * 1024 * 1024

F32 = jnp.float32
BF16 = jnp.bfloat16


def _cparams(*sem):
    return pltpu.CompilerParams(dimension_semantics=sem, vmem_limit_bytes=VMEM_LIMIT)


def _dot(a, b):
    return jnp.dot(a, b, preferred_element_type=F32)


def _dot_nt(a, b):
    return lax.dot_general(a, b, (((1,), (1,)), ((), ())), preferred_element_type=F32)


def _full(shape):
    nd = len(shape)
    return pl.BlockSpec(shape, lambda *_: (0,) * nd)


def _layer_norm(y, g, b):
    mu = jnp.mean(y, axis=-1, keepdims=True)
    d = y - mu
    var = jnp.mean(d * d, axis=-1, keepdims=True)
    return d * lax.rsqrt(var + NORM_EPS) * g + b


def _rms_norm(y, g):
    return y * lax.rsqrt(jnp.mean(y * y, axis=-1, keepdims=True) + NORM_EPS) * g


def _mla_proj_kernel(x_ref, w_in_ref, qn_ref, kvn_ref, cos_ref, sin_ref, t1_ref, wq_ref, wuk_ref, wuvt_ref,
                     ckv_ref, krkr_ref, cq_ref, q_ref, k_ref, vt_ref):
    x = x_ref[...].astype(BF16)
    z = _dot(x, w_in_ref[...])
    cq = _rms_norm(z[:, :Q_LORA], qn_ref[...])
    ckv = _rms_norm(z[:, Q_LORA:Q_LORA + KV_LORA], kvn_ref[...])
    o = Q_LORA + KV_LORA
    krkr = z[:, o:o + LANES] * cos_ref[...] + z[:, o + LANES:o + 2 * LANES] * sin_ref[...]
    ckv_ref[...] = ckv
    krkr_ref[...] = krkr
    cq_b = cq.astype(BF16)
    cq_ref[...] = cq_b
    ckv_b = ckv.astype(BF16)
    t1 = t1_ref[...]
    qz = _dot(cq_b, wq_ref[...])
    for h in range(N_HEADS):
        q_ref[h] = (qz[:, h * HEAD_PAD:(h + 1) * HEAD_PAD] * t1).astype(BF16)
    kz = _dot(ckv_b, wuk_ref[...])
    for h in range(N_HEADS):
        k_ref[h] = (kz[:, h * HEAD_PAD:(h + 1) * HEAD_PAD] + krkr).astype(BF16)
    vz = _dot_nt(wuvt_ref[...], ckv_b)
    ones = jnp.ones((V_HEAD, vz.shape[1]), BF16)
    for h in range(N_HEADS):
        val = vz[h * V_HEAD:(h + 1) * V_HEAD].astype(BF16)
        lo, hi = (val, ones) if h % 2 == 0 else (ones, val)
        vt_ref[h, 0, :V_HEAD, :] = lo
        vt_ref[h, 0, V_HEAD:, :] = hi


def _mla_proj(x, w_in, qn, kvn, cos4, sin4, t1, wq, wuk, wuvt):
    nt, d = x.shape
    t = PROJ_TILE
    row = lambda w: pl.BlockSpec((t, w), lambda i: (i, 0))
    head = pl.BlockSpec((N_HEADS, t, HEAD_PAD), lambda i: (0, i, 0))
    head_sds = jax.ShapeDtypeStruct((N_HEADS, nt, HEAD_PAD), BF16)
    return pl.pallas_call(
        _mla_proj_kernel,
        grid=(nt // t,),
        in_specs=[row(d), _full(w_in.shape), _full(qn.shape), _full(kvn.shape), row(LANES), row(LANES), row(LANES),
                  _full(wq.shape), _full(wuk.shape), _full(wuvt.shape)],
        out_specs=[row(KV_LORA), row(LANES), row(Q_LORA), head, head,
                   pl.BlockSpec((N_HEADS, 1, HEAD_PAD, t), lambda i: (0, i, 0, 0))],
        out_shape=[jax.ShapeDtypeStruct((nt, KV_LORA), F32), jax.ShapeDtypeStruct((nt, LANES), F32),
                   jax.ShapeDtypeStruct((nt, Q_LORA), BF16), head_sds, head_sds,
                   jax.ShapeDtypeStruct((N_HEADS, nt // t, HEAD_PAD, t), BF16)],
        compiler_params=_cparams("parallel"),
        name="mla_proj",
    )(x, w_in, qn, kvn, cos4, sin4, t1, wq, wuk, wuvt)


def _head_out(acc, g):
    if g % 2 == 0:
        return acc[:V_HEAD] * (1.0 / acc[V_HEAD:V_HEAD + 1])
    return acc[V_HEAD:] * (1.0 / acc[0:1])


def _attn_prompt_kernel(q_ref, k_ref, vt_ref, qm_ref, km_ref, vmt_ref, o_ref, omt_ref):
    qi = pl.program_id(2)
    blk = ATTN_BLOCK
    n_pair, tq, _ = q_ref.shape

    def scores(g, j):
        rows = pl.ds(pl.multiple_of(j * blk, blk), blk)
        return _dot_nt(k_ref[g, rows, :], q_ref[g])

    def softmax(s, m):
        m_new = jnp.maximum(m, jnp.max(s, axis=0, keepdims=True))
        return jnp.exp(s - m_new).astype(BF16), jnp.exp(m - m_new), m_new

    def values(g, j, p, alpha, acc):
        return alpha * acc + _dot(vt_ref[g, j], p)

    carry = []
    for g in range(n_pair):
        carry += [scores(g, 0), jnp.full((1, tq), -jnp.inf, F32), jnp.zeros((HEAD_PAD, tq), F32)]

    def body(j, carry):
        out = []
        for g in range(n_pair):
            s, m, acc = carry[3 * g:3 * g + 3]
            s_next = scores(g, j + 1)
            p, alpha, m = softmax(s, m)
            out += [s_next, m, values(g, j, p, alpha, acc)]
        return tuple(out)

    carry = lax.fori_loop(0, qi, body, tuple(carry))

    halves = []
    for g in range(n_pair):
        s, m, acc = carry[3 * g:3 * g + 3]
        key = lax.broadcasted_iota(jnp.int32, s.shape, 0)
        qry = lax.broadcasted_iota(jnp.int32, s.shape, 1)
        s = jnp.where(key <= qry, s, -jnp.inf)
        s_m = _dot_nt(km_ref[g], q_ref[g])
        m_new = jnp.maximum(jnp.maximum(m, jnp.max(s, axis=0, keepdims=True)), jnp.max(s_m, axis=0, keepdims=True))
        p = jnp.exp(s - m_new).astype(BF16)
        p_m = jnp.exp(s_m - m_new).astype(BF16)
        acc = jnp.exp(m - m_new) * acc + _dot(vt_ref[g, qi], p) + _dot(vmt_ref[g], p_m)
        halves.append(_head_out(acc, g))
    o_ref[...] = jnp.concatenate(halves, axis=0).T.astype(o_ref.dtype)

    @pl.when(qi == 0)
    def _():
        halves = []
        for g in range(n_pair):
            s = _dot_nt(km_ref[g], qm_ref[g])
            key = lax.broadcasted_iota(jnp.int32, s.shape, 0)
            qry = lax.broadcasted_iota(jnp.int32, s.shape, 1)
            s = jnp.where(key <= qry, s, -jnp.inf)
            p = jnp.exp(s - jnp.max(s, axis=0, keepdims=True)).astype(BF16)
            halves.append(_head_out(_dot(vmt_ref[g], p), g))
        omt_ref[...] = jnp.concatenate(halves, axis=0)


def _attn_prompt(q, k, vt, vmt, n_batch, seq):
    n_real = n_batch * seq
    blk = ATTN_BLOCK
    n_q = seq // blk
    pair = HEADS_PER_STEP
    meta_blk0 = n_real // N_META
    qspec = pl.BlockSpec((pair, blk, HEAD_PAD), lambda b, hp, qi: (hp, b * n_q + qi, 0))
    kspec = pl.BlockSpec((pair, seq, HEAD_PAD), lambda b, hp, qi: (hp, b, 0))
    vtspec = pl.BlockSpec((pair, n_q, HEAD_PAD, blk), lambda b, hp, qi: (hp, b, 0, 0))
    mspec = pl.BlockSpec((pair, N_META, HEAD_PAD), lambda b, hp, qi: (hp, meta_blk0 + b, 0))
    vmtspec = pl.BlockSpec((None, pair, HEAD_PAD, N_META), lambda b, hp, qi: (b, hp, 0, 0))
    return pl.pallas_call(
        _attn_prompt_kernel,
        grid=(n_batch, N_HEADS // pair, n_q),
        in_specs=[qspec, kspec, vtspec, mspec, mspec, vmtspec],
        out_specs=[pl.BlockSpec((blk, pair * V_HEAD), lambda b, hp, qi: (b * n_q + qi, hp)),
                   pl.BlockSpec((None, None, pair * V_HEAD, N_META), lambda b, hp, qi: (b, hp, 0, 0))],
        out_shape=[jax.ShapeDtypeStruct((n_real, N_HEADS * V_HEAD), BF16),
                   jax.ShapeDtypeStruct((n_batch, N_HEADS // pair, pair * V_HEAD, N_META), F32)],
        compiler_params=_cparams("parallel", "parallel", "arbitrary"),
        name="attn_prompt",
    )(q, k, vt, q, k, vmt)


def _sample_q_kernel(cq_ref, wq_ref, wukt_ref, t1_ref, qlat_ref, qrope_ref):
    qz = _dot(cq_ref[...], wq_ref[...])
    t1 = t1_ref[...]
    for h in range(N_HEADS):
        e = qz[:, h * HEAD_PAD:(h + 1) * HEAD_PAD] * t1
        qlat_ref[h] = _dot(e[:, :QK_NOPE].astype(BF16), wukt_ref[h]).astype(BF16)
        qrope_ref[h] = (e[:, QK_NOPE:QK_NOPE + QK_ROPE] + e[:, QK_NOPE + QK_ROPE:]).astype(BF16)


def _sample_q(cq, wq, wukt, t1):
    n = cq.shape[0]
    return pl.pallas_call(
        _sample_q_kernel,
        grid=(1,),
        in_specs=[_full(cq.shape), _full(wq.shape), _full(wukt.shape), _full(t1.shape)],
        out_specs=[_full((N_HEADS, n, KV_LORA)), _full((N_HEADS, n, QK_ROPE))],
        out_shape=[jax.ShapeDtypeStruct((N_HEADS, n, KV_LORA), BF16),
                   jax.ShapeDtypeStruct((N_HEADS, n, QK_ROPE), BF16)],
        compiler_params=_cparams("arbitrary"),
        name="sample_q",
    )(cq, wq, wukt, t1)


def _attn_sample_kernel(layer, n_pages, page, pt_ref, qlat_ref, qrope_ref, cnew_ref, rnew_ref, clat_hbm, ckr_hbm,
                        o_ref, cbuf, rbuf, cb16, s_scr, sem):
    b = pl.program_id(0)
    nb = pl.num_programs(0)
    slot = b % 2

    def copies(seq, slot_, p):
        phys = pt_ref[seq, p]
        rows = pl.ds(pl.multiple_of(p * page, page), page)
        return (pltpu.make_async_copy(clat_hbm.at[layer, phys], cbuf.at[slot_, rows, :], sem.at[0, slot_]),
                pltpu.make_async_copy(ckr_hbm.at[layer, phys], rbuf.at[slot_, :, rows], sem.at[1, slot_]))

    def fetch(seq, slot_):
        def body(p, c):
            for cp in copies(seq, slot_, p):
                cp.start()
            return c
        lax.fori_loop(0, n_pages, body, 0)

    @pl.when(b == 0)
    def _():
        fetch(0, 0)

    @pl.when(b + 1 < nb)
    def _():
        fetch(b + 1, 1 - slot)

    def wait_body(p, c):
        for cp in copies(b, slot, p):
            cp.wait()
        return c
    lax.fori_loop(0, n_pages, wait_body, 0)

    qlat = qlat_ref[0]
    qrope = qrope_ref[0]
    n_chunk = n_pages * page // KV_CHUNK
    mx = None
    for ck in range(n_chunk):
        rows = pl.ds(ck * KV_CHUNK, KV_CHUNK)
        c = cbuf[slot, rows, :].astype(BF16)
        cb16[rows, :] = c
        s = _dot_nt(qlat, c) + _dot(qrope, rbuf[slot, :, rows].astype(BF16))
        s_scr[:, rows] = s
        for i in range(KV_CHUNK // LANES):
            t = s[:, i * LANES:(i + 1) * LANES]
            mx = t if mx is None else jnp.maximum(mx, t)
    cn = cnew_ref[0].astype(BF16)
    s_n = _dot_nt(qlat, cn) + _dot_nt(qrope, rnew_ref[0].astype(BF16))
    row = lax.broadcasted_iota(jnp.int32, s_n.shape, 0)
    col = lax.broadcasted_iota(jnp.int32, s_n.shape, 1)
    s_n = jnp.where(col * N_HEADS <= row, s_n, -jnp.inf)
    m = jnp.maximum(jnp.max(mx, axis=-1, keepdims=True), jnp.max(s_n, axis=-1, keepdims=True))
    p_n = jnp.exp(s_n - m)
    l = jnp.sum(p_n, axis=-1, keepdims=True)
    acc = _dot(p_n.astype(BF16), cn)
    for ck in range(n_chunk):
        rows = pl.ds(ck * KV_CHUNK, KV_CHUNK)
        p = jnp.exp(s_scr[:, rows] - m)
        l = l + jnp.sum(p, axis=-1, keepdims=True)
        acc = acc + _dot(p.astype(BF16), cb16[rows, :])
    o_ref[0] = acc * (1.0 / l)


def _attn_sample(layer, page_table, qlat, qrope, cnew, rnew, cache_latent, cache_krope):
    n_seq, n_pages = page_table.shape
    page = cache_latent.shape[2]
    rows = qlat.shape[1]
    past = n_pages * page
    blk = lambda a: pl.BlockSpec((1,) + a.shape[1:], lambda b, pt: (b, 0, 0))
    return pl.pallas_call(
        functools.partial(_attn_sample_kernel, layer, n_pages, page),
        grid_spec=pltpu.PrefetchScalarGridSpec(
            num_scalar_prefetch=1,
            grid=(n_seq,),
            in_specs=[blk(qlat), blk(qrope), blk(cnew), blk(rnew),
                      pl.BlockSpec(memory_space=pl.ANY), pl.BlockSpec(memory_space=pl.ANY)],
            out_specs=pl.BlockSpec((1, rows, KV_LORA), lambda b, pt: (b, 0, 0)),
            scratch_shapes=[pltpu.VMEM((2, past, KV_LORA), F32), pltpu.VMEM((2, QK_ROPE, past), F32),
                            pltpu.VMEM((past, KV_LORA), BF16), pltpu.VMEM((rows, past), F32),
                            pltpu.SemaphoreType.DMA((2, 2))],
        ),
        out_shape=jax.ShapeDtypeStruct((n_seq, rows, KV_LORA), F32),
        compiler_params=_cparams("arbitrary"),
        name="attn_sample",
    )(page_table, qlat, qrope, cnew, rnew, cache_latent, cache_krope)


def _sample_ov_kernel(o_ref, wuv_ref, out_ref):
    for hp in range(N_HEADS // 2):
        a = _dot(o_ref[2 * hp], wuv_ref[2 * hp]) + _dot(o_ref[2 * hp + 1], wuv_ref[2 * hp + 1])
        out_ref[:, hp * LANES:(hp + 1) * LANES] = a.astype(out_ref.dtype)


def _sample_ov(o, wuv):
    n = o.shape[1]
    return pl.pallas_call(
        _sample_ov_kernel,
        grid=(1,),
        in_specs=[_full(o.shape), _full(wuv.shape)],
        out_specs=_full((n, N_HEADS * V_HEAD)),
        out_shape=jax.ShapeDtypeStruct((n, N_HEADS * V_HEAD), BF16),
        compiler_params=_cparams("arbitrary"),
        name="sample_ov",
    )(o, wuv)


def _proj_ln_kernel(alpha, x_ref, y_ref, w_ref, g_ref, b_ref, o_ref):
    y = alpha * x_ref[...] + _dot(y_ref[...], w_ref[...])
    o_ref[...] = _layer_norm(y, g_ref[...], b_ref[...])


def _proj_ln(alpha, x, y, w, g, b):
    nt, d = x.shape
    t = ROW_TILE
    row = pl.BlockSpec((t, d), lambda i: (i, 0))
    return pl.pallas_call(
        functools.partial(_proj_ln_kernel, alpha),
        grid=(nt // t,),
        in_specs=[row, pl.BlockSpec((t, y.shape[1]), lambda i: (i, 0)), _full(w.shape), _full(g.shape), _full(b.shape)],
        out_specs=row,
        out_shape=jax.ShapeDtypeStruct((nt, d), F32),
        compiler_params=_cparams("parallel"),
        name="proj_ln",
    )(x, y, w, g, b)


def _ffn_kernel(alpha, x_ref, wgu_ref, wd_ref, g_ref, b_ref, o_ref, acc_ref):
    x = x_ref[...]
    xb = x.astype(BF16)
    d_ff = wd_ref.shape[0]
    for c0 in range(0, d_ff, FFN_CHUNK):
        c1 = min(c0 + FFN_CHUNK, d_ff)
        gate = _dot(xb, wgu_ref[:, c0:c1])
        up = _dot(xb, wgu_ref[:, d_ff + c0:d_ff + c1])
        h = (gate * jax.nn.sigmoid(gate) * up).astype(BF16)
        part = _dot(h, wd_ref[c0:c1, :])
        if c0 == 0:
            acc_ref[...] = part
        else:
            acc_ref[...] += part
    o_ref[...] = _layer_norm(alpha * x + acc_ref[...], g_ref[...], b_ref[...])


def _ffn(alpha, layer, x, wgu, wd, g, b):
    nt, d = x.shape
    t = ROW_TILE
    row = pl.BlockSpec((t, d), lambda i: (i, 0))
    once = lambda a: pl.BlockSpec((None,) + a.shape[1:], lambda i: (layer, 0, 0), pipeline_mode=pl.Buffered(1))
    return pl.pallas_call(
        functools.partial(_ffn_kernel, alpha),
        grid=(nt // t,),
        in_specs=[row, once(wgu), once(wd), _full(g.shape), _full(b.shape)],
        out_specs=row,
        out_shape=jax.ShapeDtypeStruct((nt, d), F32),
        scratch_shapes=[pltpu.VMEM((t, d), F32)],
        compiler_params=_cparams("parallel"),
        name="ffn",
    )(x, wgu, wd, g, b)


def _gelu_tanh(x):
    return 0.5 * x * (1.0 + jnp.tanh(0.7978845608028654 * (x + 0.044715 * (x * x * x))))


def _lru_in_kernel(x_ref, w_ref, xr_ref, gg_ref):
    z = _dot(x_ref[...].astype(BF16), w_ref[...])
    w = xr_ref.shape[1]
    xr_ref[...] = z[:, :w]
    gg_ref[...] = _gelu_tanh(z[:, w:]).astype(gg_ref.dtype)


def _lru_in(x, w):
    nt, d = x.shape
    t = ROW_TILE
    wid = w.shape[1] // 2
    row = lambda n: pl.BlockSpec((t, n), lambda i: (i, 0))
    return pl.pallas_call(
        _lru_in_kernel,
        grid=(nt // t,),
        in_specs=[row(d), _full(w.shape)],
        out_specs=[row(wid), row(wid)],
        out_shape=[jax.ShapeDtypeStruct((nt, wid), F32), jax.ShapeDtypeStruct((nt, wid), BF16)],
        compiler_params=_cparams("parallel"),
        name="lru_in",
    )(x, w)


def _lru_gates(xc, wai, ba, bi, sp):
    zg = _dot(xc.astype(BF16), wai)
    r = jax.nn.sigmoid(zg[:, :LANES] + ba)
    i = jax.nn.sigmoid(zg[:, LANES:] + bi)
    log_a = (-LRU_C) * r * sp
    a = jnp.exp(log_a)
    th = jnp.tanh(log_a)
    one_minus_a2 = -2.0 * th / (1.0 - th)
    u = jnp.sqrt(jnp.maximum(one_minus_a2, 0.0)) * (i * xc)
    return a, u


def _softplus(y):
    return jnp.maximum(y, 0.0) + jnp.log1p(jnp.exp(-jnp.abs(y)))


def _shift_rows(v, d, fill):
    row = lax.broadcasted_iota(jnp.int32, v.shape, 0)
    return jnp.where(row >= d, pltpu.roll(v, d, axis=0), fill)


def _scan_block(xr, gg, tail, h_in, cw, cb, wai, ba, bi, lam):
    n = xr.shape[0]
    xp = jnp.concatenate([tail, xr], axis=0)
    xc = cb + cw[CONV_W - 1:CONV_W] * xr
    for j in range(1, CONV_W):
        xc = xc + cw[CONV_W - 1 - j:CONV_W - j] * pltpu.roll(xp, j, axis=0)[SUBLANES:]
    a, u = _lru_gates(xc, wai, ba, bi, _softplus(-lam))
    d = 1
    while d < n:
        u = u + a * _shift_rows(u, d, 0.0)
        a = a * _shift_rows(a, d, 1.0)
        d *= 2
    hs = a * h_in + u
    return (hs * gg.astype(F32)), hs[n - 1:n], xr[n - SUBLANES:]


def _lru_scan_prompt_kernel(xr_ref, gg_ref, xrm_ref, ggm_ref, cw_ref, cb_ref, wai_ref, ba_ref, bi_ref, lam_ref,
                            y_ref, ym_ref, hT_ref, h_scr, tail_scr):
    c = pl.program_id(1)

    def run(xr_r, gg_r, y_r):
        for n in range(LRU_BLOCKS):
            ln = slice(n * LANES, (n + 1) * LANES)
            y, h, tail = _scan_block(xr_r[:, ln], gg_r[:, ln], tail_scr[:, ln], h_scr[0:1, ln], cw_ref[:, ln],
                                     cb_ref[:, ln], wai_ref[n], ba_ref[:, ln], bi_ref[:, ln], lam_ref[:, ln])
            y_r[:, ln] = y.astype(y_r.dtype)
            h_scr[:, ln] = jnp.broadcast_to(h, (SUBLANES, LANES))
            tail_scr[:, ln] = tail

    @pl.when(c == 0)
    def _():
        h_scr[...] = jnp.zeros_like(h_scr)
        tail_scr[...] = jnp.zeros_like(tail_scr)
        run(xrm_ref, ggm_ref, ym_ref)

    run(xr_ref, gg_ref, y_ref)
    hT_ref[...] = h_scr[...]


def _lru_scan_prompt(xr, gg, n_batch, seq, cw, cb, wai, ba, bi, lam):
    n_real = n_batch * seq
    w = xr.shape[1]
    t = SCAN_CHUNK
    n_c = seq // t
    meta_blk0 = n_real // N_META
    real = pl.BlockSpec((t, w), lambda b, c: (b * n_c + c, 0))
    meta = pl.BlockSpec((N_META, w), lambda b, c: (meta_blk0 + b, 0))
    return pl.pallas_call(
        _lru_scan_prompt_kernel,
        grid=(n_batch, n_c),
        in_specs=[real, real, meta, meta, _full(cw.shape), _full(cb.shape), _full(wai.shape), _full(ba.shape),
                  _full(bi.shape), _full(lam.shape)],
        out_specs=[real, pl.BlockSpec((N_META, w), lambda b, c: (b, 0)), pl.BlockSpec((SUBLANES, w), lambda b, c: (b, 0))],
        out_shape=[jax.ShapeDtypeStruct((n_real, w), BF16), jax.ShapeDtypeStruct((n_batch * N_META, w), BF16),
                   jax.ShapeDtypeStruct((n_batch * SUBLANES, w), F32)],
        scratch_shapes=[pltpu.VMEM((SUBLANES, w), F32), pltpu.VMEM((SUBLANES, w), F32)],
        compiler_params=_cparams("parallel", "arbitrary"),
        name="lru_scan_prompt",
    )(xr, gg, xr, gg, cw, cb, wai, ba, bi, lam)


def _lru_scan_sample_kernel(xr_ref, gg_ref, buf_ref, h0_ref, cw_ref, cb_ref, wai_ref, ba_ref, bi_ref, lam_ref,
                            y_ref, hT_ref):
    n_t = xr_ref.shape[0]
    for n in range(LRU_BLOCKS):
        ln = slice(n * LANES, (n + 1) * LANES)
        rows = [buf_ref[j, :, ln] for j in range(CONV_W - 1)] + [xr_ref[t, :, ln] for t in range(n_t)]
        sp = _softplus(-lam_ref[:, ln])
        h = h0_ref[:, ln]
        for t in range(n_t):
            xc = cb_ref[:, ln]
            for j in range(CONV_W):
                xc = xc + cw_ref[j:j + 1, ln] * rows[t + j]
            a, u = _lru_gates(xc, wai_ref[n], ba_ref[:, ln], bi_ref[:, ln], sp)
            h = a * h + u
            y_ref[t, :, ln] = (h * gg_ref[t, :, ln].astype(F32)).astype(y_ref.dtype)
        hT_ref[:, ln] = h


def _lru_scan_sample(xr, gg, buf, h0, cw, cb, wai, ba, bi, lam):
    args = (xr, gg, buf, h0, cw, cb, wai, ba, bi, lam)
    return pl.pallas_call(
        _lru_scan_sample_kernel,
        grid=(1,),
        in_specs=[_full(a.shape) for a in args],
        out_specs=[_full(xr.shape), _full(h0.shape)],
        out_shape=[jax.ShapeDtypeStruct(xr.shape, BF16), jax.ShapeDtypeStruct(h0.shape, F32)],
        compiler_params=_cparams("arbitrary"),
        name="lru_scan_sample",
    )(*args)


def _rot_cols(w):
    half = w.shape[-1] // 2
    return jnp.concatenate([-w[..., half:], w[..., :half]], axis=-1)


def kernel(x_prompt, x_sample, cache_latent, cache_krope, page_table, state_lru_h, state_conv, meta_tokens,
           mla_w_in, mla_q_norm, mla_w_q_up, mla_kv_norm, mla_w_uk, mla_w_uv, mla_w_o,
           lru_w_in, lru_conv_w, lru_conv_b, lru_w_a, lru_b_a, lru_w_i, lru_b_i, lru_lambda, lru_w_o,
           ffn_w_gu, ffn_w_down, ln_g, ln_b):
    n_b, seq, d = x_prompt.shape
    n_s, n_t = x_sample.shape[:2]
    depth = ffn_w_gu.shape[0]
    d_ff = ffn_w_down.shape[1]
    alpha = (2 * depth) ** 0.25
    past_len = page_table.shape[1] * cache_latent.shape[2]
    n_real = n_b * seq
    n_meta = n_b * N_META
    n_samp = n_s * n_t
    assert seq % ATTN_BLOCK == 0 and seq % SCAN_CHUNK == 0 and n_real % ROW_TILE == 0 and PROJ_TILE == ATTN_BLOCK
    assert n_meta + n_samp <= ROW_TILE and d_ff % LANES == 0
    n_pad = ROW_TILE - n_meta - n_samp
    nt = n_real + ROW_TILE
    s0, s1 = n_real + n_meta, n_real + n_meta + n_samp

    x = jnp.concatenate([x_prompt.reshape(n_real, d),
                         jnp.broadcast_to(meta_tokens[None], (n_b, N_META, d)).reshape(n_meta, d),
                         x_sample.transpose(1, 0, 2).reshape(n_samp, d),
                         jnp.zeros((n_pad, d), F32)], axis=0)

    pos = jnp.concatenate([jnp.tile(N_META + jnp.arange(seq), n_b), jnp.tile(jnp.arange(N_META), n_b),
                           jnp.repeat(past_len + jnp.arange(n_t), n_s), jnp.zeros((n_pad,), jnp.int32)])
    half = QK_ROPE // 2
    freqs = ROPE_THETA ** (-jnp.arange(half, dtype=F32) / half)
    ang = pos.astype(F32)[:, None] * freqs[None, :]
    cos2 = jnp.tile(jnp.cos(ang), (1, 2))
    sin2 = jnp.tile(jnp.sin(ang), (1, 2))
    zeros64 = jnp.zeros((nt, QK_NOPE), F32)
    cos4 = jnp.concatenate([zeros64, cos2, cos2], axis=1)
    sin4 = jnp.concatenate([zeros64, sin2, sin2], axis=1)
    t1 = jnp.concatenate([jnp.full((nt, QK_NOPE), ATTN_SCALE, F32), cos2 * ATTN_SCALE, sin2 * ATTN_SCALE], axis=1)

    even = (jnp.arange(N_HEADS) % 2 == 0)[None, :, None]
    ffn_wgu = ffn_w_gu.astype(BF16)
    ffn_wd = ffn_w_down.astype(BF16)
    cache_krope_t = cache_krope.transpose(0, 1, 3, 2)

    lat_p, kr_p, lat_s, kr_s, hp_l, cp_l, hs_l, cs_l = [], [], [], [], [], [], [], []
    for i in range(depth):
        g0, b0 = ln_g[i, 0][None], ln_b[i, 0][None]
        g1, b1 = ln_g[i, 1][None], ln_b[i, 1][None]
        if i % 2 == 0:
            a = i // 2
            w_in = mla_w_in[a]
            wr = w_in[:, Q_LORA + KV_LORA:]
            z64 = jnp.zeros((d, QK_NOPE), F32)
            w_in_ext = jnp.concatenate([w_in[:, :Q_LORA + KV_LORA], z64, wr, wr, z64, _rot_cols(wr), _rot_cols(wr)],
                                       axis=1).astype(BF16)
            wq3 = mla_w_q_up[a].reshape(Q_LORA, N_HEADS, QK_NOPE + QK_ROPE)
            wq_ext = jnp.concatenate([wq3, _rot_cols(wq3[..., QK_NOPE:])], axis=-1).reshape(Q_LORA, -1).astype(BF16)
            wuk = mla_w_uk[a]
            wuk_pad = jnp.concatenate([wuk, jnp.zeros_like(wuk)], axis=-1).reshape(KV_LORA, -1).astype(BF16)
            wuv = mla_w_uv[a]
            zv = jnp.zeros_like(wuv)
            wuv_pad3 = jnp.where(even, jnp.concatenate([wuv, zv], -1), jnp.concatenate([zv, wuv], -1)).astype(BF16)
            wuvt = wuv.reshape(KV_LORA, -1).T.astype(BF16)

            ckv, krkr, cq, q, k, vt = _mla_proj(x, w_in_ext, mla_q_norm[a][None], mla_kv_norm[a][None],
                                                cos4, sin4, t1, wq_ext, wuk_pad, wuvt)
            kr = krkr[:, QK_NOPE:QK_NOPE + QK_ROPE]
            vmt = vt[:, n_real // ATTN_BLOCK, :, :n_meta].reshape(N_HEADS, HEAD_PAD, n_b, N_META).transpose(2, 0, 1, 3)
            o_real, o_meta_t = _attn_prompt(q, k, vt, vmt, n_b, seq)
            o_meta = o_meta_t.transpose(0, 3, 1, 2).reshape(n_meta, d).astype(BF16)

            wukt = wuk.transpose(1, 2, 0).astype(BF16)
            qlat, qrope = _sample_q(cq[s0:s1], wq_ext, wukt, t1[s0:s1])
            per_seq = lambda t: t.reshape(N_HEADS, n_t, n_s, -1).transpose(2, 1, 0, 3).reshape(n_s, n_t * N_HEADS, -1)
            new_rows = lambda t: jnp.pad(t.reshape(n_t, n_s, -1).transpose(1, 0, 2),
                                         ((0, 0), (0, SUBLANES - n_t), (0, 0)))
            o_lat = _attn_sample(a, page_table, per_seq(qlat), per_seq(qrope), new_rows(ckv[s0:s1]),
                                 new_rows(kr[s0:s1]), cache_latent, cache_krope_t)
            o_lat = o_lat.reshape(n_s, n_t, N_HEADS, KV_LORA).transpose(2, 1, 0, 3).reshape(N_HEADS, n_samp, KV_LORA)
            o_samp = _sample_ov(o_lat.astype(BF16), wuv_pad3.transpose(1, 0, 2))

            mix_in = jnp.concatenate([o_real, o_meta, o_samp, jnp.zeros((n_pad, d), BF16)], axis=0)
            w_o = mla_w_o[a].astype(BF16)

            tok = lambda t: jnp.concatenate([t[n_real:s0].reshape(n_b, N_META, -1), t[:n_real].reshape(n_b, seq, -1)], 1)
            samp = lambda t: t[s0:s1].reshape(n_t, n_s, -1).transpose(1, 0, 2)
            lat_p.append(tok(ckv))
            kr_p.append(tok(kr))
            lat_s.append(samp(ckv))
            kr_s.append(samp(kr))
        else:
            lb = i // 2
            xr, gg = _lru_in(x, lru_w_in[lb].astype(BF16))
            wai = jnp.concatenate([lru_w_a[lb], lru_w_i[lb]], axis=-1).astype(BF16)
            lw = (lru_conv_w[lb], lru_conv_b[lb][None], wai, lru_b_a[lb][None], lru_b_i[lb][None],
                  lru_lambda[lb][None])
            y_real, y_meta, h_p = _lru_scan_prompt(xr, gg, n_b, seq, *lw)
            xr_s = xr[s0:s1].reshape(n_t, n_s, -1)
            y_samp, h_s = _lru_scan_sample(xr_s, gg[s0:s1].reshape(n_t, n_s, -1), state_conv[lb].transpose(1, 0, 2),
                                           state_lru_h[lb], *lw)
            mix_in = jnp.concatenate([y_real, y_meta, y_samp.reshape(n_samp, -1), jnp.zeros((n_pad, d), BF16)], axis=0)
            w_o = lru_w_o[lb].astype(BF16)

            hp_l.append(h_p.reshape(n_b, SUBLANES, -1)[:, 0])
            cp_l.append(jnp.stack([xr[(bb + 1) * seq - (CONV_W - 1):(bb + 1) * seq] for bb in range(n_b)]))
            hs_l.append(h_s)
            cs_l.append(jnp.concatenate([state_conv[lb].transpose(1, 0, 2), xr_s], axis=0)[-(CONV_W - 1):]
                        .transpose(1, 0, 2))

        x = _proj_ln(alpha, x, mix_in, w_o, g0, b0)
        x = _ffn(alpha, i, x, ffn_wgu, ffn_wd, g1, b1)

    y_prompt = x[:n_real].reshape(n_b, seq, d)
    y_sample = x[s0:s1].reshape(n_t, n_s, d).transpose(1, 0, 2)
    return (y_prompt, y_sample, jnp.stack(lat_p), jnp.stack(kr_p), jnp.stack(lat_s), jnp.stack(kr_s),
            jnp.stack(hp_l), jnp.stack(cp_l), jnp.stack(hs_l), jnp.stack(cs_l))
```

```python
import functools

import jax
import jax.numpy as jnp
from jax import lax
from jax.experimental import pallas as pl
from jax.experimental.pallas import tpu as pltpu

N_META = 16
N_HEADS = 16
Q_LORA = 384
KV_LORA = 256
QK_NOPE = 64
QK_ROPE = 32
V_HEAD = 64
ROPE_THETA = 10000.0
ATTN_SCALE = (QK_NOPE + QK_ROPE) ** -0.5
LRU_BLOCKS = 8
CONV_W = 4
LRU_C = 8.0
NORM_EPS = 1e-5

LANES = 128
SUBLANES = 8
HEAD_PAD = LANES
ROW_TILE = 1024
PROJ_TILE = 512
ATTN_BLOCK = 512
HEADS_PER_STEP = 4
SCAN_CHUNK = 512
KV_CHUNK = 1024
FFN_CHUNK = 512
VMEM_LIMIT = 56 * 1024 * 1024

F32 = jnp.float32
BF16 = jnp.bfloat16


def _cparams(*sem):
    return pltpu.CompilerParams(dimension_semantics=sem, vmem_limit_bytes=VMEM_LIMIT)


def _dot(a, b):
    return jnp.dot(a, b, preferred_element_type=F32)


def _dot_nt(a, b):
    return lax.dot_general(a, b, (((1,), (1,)), ((), ())), preferred_element_type=F32)


def _full(shape):
    nd = len(shape)
    return pl.BlockSpec(shape, lambda *_: (0,) * nd)


def _layer_norm(y, g, b):
    mu = jnp.mean(y, axis=-1, keepdims=True)
    d = y - mu
    var = jnp.mean(d * d, axis=-1, keepdims=True)
    return d * lax.rsqrt(var + NORM_EPS) * g + b


def _rms_norm(y, g):
    return y * lax.rsqrt(jnp.mean(y * y, axis=-1, keepdims=True) + NORM_EPS) * g


def _mla_proj_kernel(x_ref, w_in_ref, qn_ref, kvn_ref, cos_ref, sin_ref, t1_ref, wq_ref, wuk_ref, wuvt_ref,
                     ckv_ref, krkr_ref, cq_ref, q_ref, k_ref, vt_ref):
    x = x_ref[...].astype(BF16)
    z = _dot(x, w_in_ref[...])
    cq = _rms_norm(z[:, :Q_LORA], qn_ref[...])
    ckv = _rms_norm(z[:, Q_LORA:Q_LORA + KV_LORA], kvn_ref[...])
    o = Q_LORA + KV_LORA
    krkr = z[:, o:o + LANES] * cos_ref[...] + z[:, o + LANES:o + 2 * LANES] * sin_ref[...]
    ckv_ref[...] = ckv
    krkr_ref[...] = krkr
    cq_b = cq.astype(BF16)
    cq_ref[...] = cq_b
    ckv_b = ckv.astype(BF16)
    t1 = t1_ref[...]
    qz = _dot(cq_b, wq_ref[...])
    for h in range(N_HEADS):
        q_ref[h] = (qz[:, h * HEAD_PAD:(h + 1) * HEAD_PAD] * t1).astype(BF16)
    kz = _dot(ckv_b, wuk_ref[...])
    for h in range(N_HEADS):
        k_ref[h] = (kz[:, h * HEAD_PAD:(h + 1) * HEAD_PAD] + krkr).astype(BF16)
    vz = _dot_nt(wuvt_ref[...], ckv_b)
    ones = jnp.ones((V_HEAD, vz.shape[1]), BF16)
    for h in range(N_HEADS):
        val = vz[h * V_HEAD:(h + 1) * V_HEAD].astype(BF16)
        lo, hi = (val, ones) if h % 2 == 0 else (ones, val)
        vt_ref[h, 0, :V_HEAD, :] = lo
        vt_ref[h, 0, V_HEAD:, :] = hi


def _mla_proj(x, w_in, qn, kvn, cos4, sin4, t1, wq, wuk, wuvt):
    nt, d = x.shape
    t = PROJ_TILE
    row = lambda w: pl.BlockSpec((t, w), lambda i: (i, 0))
    head = pl.BlockSpec((N_HEADS, t, HEAD_PAD), lambda i: (0, i, 0))
    head_sds = jax.ShapeDtypeStruct((N_HEADS, nt, HEAD_PAD), BF16)
    return pl.pallas_call(
        _mla_proj_kernel,
        grid=(nt // t,),
        in_specs=[row(d), _full(w_in.shape), _full(qn.shape), _full(kvn.shape), row(LANES), row(LANES), row(LANES),
                  _full(wq.shape), _full(wuk.shape), _full(wuvt.shape)],
        out_specs=[row(KV_LORA), row(LANES), row(Q_LORA), head, head,
                   pl.BlockSpec((N_HEADS, 1, HEAD_PAD, t), lambda i: (0, i, 0, 0))],
        out_shape=[jax.ShapeDtypeStruct((nt, KV_LORA), F32), jax.ShapeDtypeStruct((nt, LANES), F32),
                   jax.ShapeDtypeStruct((nt, Q_LORA), BF16), head_sds, head_sds,
                   jax.ShapeDtypeStruct((N_HEADS, nt // t, HEAD_PAD, t), BF16)],
        compiler_params=_cparams("parallel"),
        name="mla_proj",
    )(x, w_in, qn, kvn, cos4, sin4, t1, wq, wuk, wuvt)


def _head_out(acc, g):
    if g % 2 == 0:
        return acc[:V_HEAD] * (1.0 / acc[V_HEAD:V_HEAD + 1])
    return acc[V_HEAD:] * (1.0 / acc[0:1])


def _attn_prompt_kernel(q_ref, k_ref, vt_ref, qm_ref, km_ref, vmt_ref, o_ref, omt_ref):
    qi = pl.program_id(2)
    blk = ATTN_BLOCK
    n_pair, tq, _ = q_ref.shape

    def scores(g, j):
        rows = pl.ds(pl.multiple_of(j * blk, blk), blk)
        return _dot_nt(k_ref[g, rows, :], q_ref[g])

    def softmax(s, m):
        m_new = jnp.maximum(m, jnp.max(s, axis=0, keepdims=True))
        return jnp.exp(s - m_new).astype(BF16), jnp.exp(m - m_new), m_new

    def values(g, j, p, alpha, acc):
        return alpha * acc + _dot(vt_ref[g, j], p)

    carry = []
    for g in range(n_pair):
        s = scores(g, qi)
        key = lax.broadcasted_iota(jnp.int32, s.shape, 0)
        qry = lax.broadcasted_iota(jnp.int32, s.shape, 1)
        s = jnp.where(key <= qry, s, -jnp.inf)
        s_m = _dot_nt(km_ref[g], q_ref[g])
        s_first = scores(g, 0)
        m = jnp.maximum(jnp.max(s, axis=0, keepdims=True), jnp.max(s_m, axis=0, keepdims=True))
        p = jnp.exp(s - m).astype(BF16)
        p_m = jnp.exp(s_m - m).astype(BF16)
        carry += [s_first, m, _dot(vt_ref[g, qi], p) + _dot(vmt_ref[g], p_m)]

    def body(j, carry):
        out = []
        for g in range(n_pair):
            s, m, acc = carry[3 * g:3 * g + 3]
            s_next = scores(g, j + 1)
            p, alpha, m = softmax(s, m)
            out += [s_next, m, values(g, j, p, alpha, acc)]
        return tuple(out)

    carry = lax.fori_loop(0, qi, body, tuple(carry))
    halves = [_head_out(carry[3 * g + 2], g) for g in range(n_pair)]
    o_ref[...] = jnp.concatenate(halves, axis=0).T.astype(o_ref.dtype)

    @pl.when(qi == 0)
    def _():
        halves = []
        for g in range(n_pair):
            s = _dot_nt(km_ref[g], qm_ref[g])
            key = lax.broadcasted_iota(jnp.int32, s.shape, 0)
            qry = lax.broadcasted_iota(jnp.int32, s.shape, 1)
            s = jnp.where(key <= qry, s, -jnp.inf)
            p = jnp.exp(s - jnp.max(s, axis=0, keepdims=True)).astype(BF16)
            halves.append(_head_out(_dot(vmt_ref[g], p), g))
        omt_ref[...] = jnp.concatenate(halves, axis=0)


def _attn_prompt(q, k, vt, qs, ks, vmt, n_batch, seq):
    n_real = n_batch * seq
    blk = ATTN_BLOCK
    n_q = seq // blk
    pair = HEADS_PER_STEP
    qspec = pl.BlockSpec((pair, blk, HEAD_PAD), lambda b, hp, qi: (hp, b * n_q + qi, 0))
    kspec = pl.BlockSpec((pair, seq, HEAD_PAD), lambda b, hp, qi: (hp, b, 0))
    vtspec = pl.BlockSpec((pair, n_q, HEAD_PAD, blk), lambda b, hp, qi: (hp, b, 0, 0))
    mspec = pl.BlockSpec((pair, N_META, HEAD_PAD), lambda b, hp, qi: (hp, b, 0))
    vmtspec = pl.BlockSpec((None, pair, HEAD_PAD, N_META), lambda b, hp, qi: (b, hp, 0, 0))
    return pl.pallas_call(
        _attn_prompt_kernel,
        grid=(n_batch, N_HEADS // pair, n_q),
        in_specs=[qspec, kspec, vtspec, mspec, mspec, vmtspec],
        out_specs=[pl.BlockSpec((blk, pair * V_HEAD), lambda b, hp, qi: (b * n_q + qi, hp)),
                   pl.BlockSpec((None, None, pair * V_HEAD, N_META), lambda b, hp, qi: (b, hp, 0, 0))],
        out_shape=[jax.ShapeDtypeStruct((n_real, N_HEADS * V_HEAD), BF16),
                   jax.ShapeDtypeStruct((n_batch, N_HEADS // pair, pair * V_HEAD, N_META), F32)],
        compiler_params=_cparams("parallel", "parallel", "arbitrary"),
        name="attn_prompt",
    )(q, k, vt, qs, ks, vmt)


def _sample_q_kernel(cq_ref, wq_ref, wukt_ref, t1_ref, qlat_ref, qrope_ref):
    qz = _dot(cq_ref[...], wq_ref[...])
    t1 = t1_ref[...]
    for h in range(N_HEADS):
        e = qz[:, h * HEAD_PAD:(h + 1) * HEAD_PAD] * t1
        qlat_ref[h] = _dot(e[:, :QK_NOPE].astype(BF16), wukt_ref[h]).astype(BF16)
        qrope_ref[h] = (e[:, QK_NOPE:QK_NOPE + QK_ROPE] + e[:, QK_NOPE + QK_ROPE:]).astype(BF16)


def _sample_q(cq, wq, wukt, t1):
    n = cq.shape[0]
    return pl.pallas_call(
        _sample_q_kernel,
        grid=(1,),
        in_specs=[_full(cq.shape), _full(wq.shape), _full(wukt.shape), _full(t1.shape)],
        out_specs=[_full((N_HEADS, n, KV_LORA)), _full((N_HEADS, n, QK_ROPE))],
        out_shape=[jax.ShapeDtypeStruct((N_HEADS, n, KV_LORA), BF16),
                   jax.ShapeDtypeStruct((N_HEADS, n, QK_ROPE), BF16)],
        compiler_params=_cparams("arbitrary"),
        name="sample_q",
    )(cq, wq, wukt, t1)


def _attn_sample_kernel(layer, n_pages, page, pt_ref, qlat_ref, qrope_ref, cnew_ref, rnew_ref, clat_hbm, ckr_hbm,
                        o_ref, cbuf, rbuf, cb16, s_scr, sem):
    b = pl.program_id(0)
    nb = pl.num_programs(0)
    slot = b % 2

    def copies(phys, slot_, p):
        rows = pl.ds(p * page, page)
        return (pltpu.make_async_copy(clat_hbm.at[layer, phys], cbuf.at[slot_, rows, :], sem.at[0, slot_]),
                pltpu.make_async_copy(ckr_hbm.at[layer, phys], rbuf.at[slot_, :, rows], sem.at[1, slot_]))

    def fetch(seq, slot_):
        for p in range(n_pages):
            for cp in copies(pt_ref[seq, p], slot_, p):
                cp.start()

    def wait(slot_):
        for p in range(n_pages):
            for cp in copies(0, slot_, p):
                cp.wait()

    @pl.when(b == 0)
    def _():
        fetch(0, 0)

    fetch(jnp.minimum(b + 1, nb - 1), 1 - slot)
    wait(slot)

    qlat = qlat_ref[0]
    qrope = qrope_ref[0]
    n_chunk = n_pages * page // KV_CHUNK
    mx = None
    for ck in range(n_chunk):
        rows = pl.ds(ck * KV_CHUNK, KV_CHUNK)
        c = cbuf[slot, rows, :].astype(BF16)
        cb16[rows, :] = c
        s = _dot_nt(qlat, c) + _dot(qrope, rbuf[slot, :, rows].astype(BF16))
        s_scr[:, rows] = s
        for i in range(KV_CHUNK // LANES):
            t = s[:, i * LANES:(i + 1) * LANES]
            mx = t if mx is None else jnp.maximum(mx, t)
    cn = cnew_ref[0].astype(BF16)
    s_n = _dot_nt(qlat, cn) + _dot_nt(qrope, rnew_ref[0].astype(BF16))
    row = lax.broadcasted_iota(jnp.int32, s_n.shape, 0)
    col = lax.broadcasted_iota(jnp.int32, s_n.shape, 1)
    s_n = jnp.where(col * N_HEADS <= row, s_n, -jnp.inf)
    m = jnp.maximum(jnp.max(mx, axis=-1, keepdims=True), jnp.max(s_n, axis=-1, keepdims=True))
    p_n = jnp.exp(s_n - m)
    l = jnp.sum(p_n, axis=-1, keepdims=True)
    acc = _dot(p_n.astype(BF16), cn)
    for ck in range(n_chunk):
        rows = pl.ds(ck * KV_CHUNK, KV_CHUNK)
        p = jnp.exp(s_scr[:, rows] - m)
        l = l + jnp.sum(p, axis=-1, keepdims=True)
        acc = acc + _dot(p.astype(BF16), cb16[rows, :])
    o_ref[0] = acc * (1.0 / l)

    @pl.when(b == nb - 1)
    def _():
        wait(1 - slot)


def _attn_sample(layer, page_table, qlat, qrope, cnew, rnew, cache_latent, cache_krope):
    n_seq, n_pages = page_table.shape
    page = cache_latent.shape[2]
    rows = qlat.shape[1]
    past = n_pages * page
    blk = lambda a: pl.BlockSpec((1,) + a.shape[1:], lambda b, pt: (b, 0, 0))
    return pl.pallas_call(
        functools.partial(_attn_sample_kernel, layer, n_pages, page),
        grid_spec=pltpu.PrefetchScalarGridSpec(
            num_scalar_prefetch=1,
            grid=(n_seq,),
            in_specs=[blk(qlat), blk(qrope), blk(cnew), blk(rnew),
                      pl.BlockSpec(memory_space=pl.ANY), pl.BlockSpec(memory_space=pl.ANY)],
            out_specs=pl.BlockSpec((1, rows, KV_LORA), lambda b, pt: (b, 0, 0)),
            scratch_shapes=[pltpu.VMEM((2, past, KV_LORA), F32), pltpu.VMEM((2, QK_ROPE, past), F32),
                            pltpu.VMEM((past, KV_LORA), BF16), pltpu.VMEM((rows, past), F32),
                            pltpu.SemaphoreType.DMA((2, 2))],
        ),
        out_shape=jax.ShapeDtypeStruct((n_seq, rows, KV_LORA), F32),
        compiler_params=_cparams("arbitrary"),
        name="attn_sample",
    )(page_table, qlat, qrope, cnew, rnew, cache_latent, cache_krope)


def _sample_ov_kernel(o_ref, wuv_ref, out_ref):
    for hp in range(N_HEADS // 2):
        a = _dot(o_ref[2 * hp], wuv_ref[2 * hp]) + _dot(o_ref[2 * hp + 1], wuv_ref[2 * hp + 1])
        out_ref[:, hp * LANES:(hp + 1) * LANES] = a.astype(out_ref.dtype)


def _sample_ov(o, wuv):
    n = o.shape[1]
    return pl.pallas_call(
        _sample_ov_kernel,
        grid=(1,),
        in_specs=[_full(o.shape), _full(wuv.shape)],
        out_specs=_full((n, N_HEADS * V_HEAD)),
        out_shape=jax.ShapeDtypeStruct((n, N_HEADS * V_HEAD), BF16),
        compiler_params=_cparams("arbitrary"),
        name="sample_ov",
    )(o, wuv)


def _proj_ln_kernel(alpha, x_ref, y_ref, w_ref, g_ref, b_ref, o_ref):
    y = alpha * x_ref[...] + _dot(y_ref[...], w_ref[...])
    o_ref[...] = _layer_norm(y, g_ref[...], b_ref[...])


def _proj_ln(alpha, x, y, w, g, b):
    nt, d = x.shape
    t = ROW_TILE
    row = pl.BlockSpec((t, d), lambda i: (i, 0))
    return pl.pallas_call(
        functools.partial(_proj_ln_kernel, alpha),
        grid=(nt // t,),
        in_specs=[row, pl.BlockSpec((t, y.shape[1]), lambda i: (i, 0)), _full(w.shape), _full(g.shape), _full(b.shape)],
        out_specs=row,
        out_shape=jax.ShapeDtypeStruct((nt, d), F32),
        compiler_params=_cparams("parallel"),
        name="proj_ln",
    )(x, y, w, g, b)


def _ffn_kernel(alpha, x_ref, wgu_ref, wd_ref, g_ref, b_ref, o_ref, acc_ref):
    x = x_ref[...]
    xb = x.astype(BF16)
    d_ff = wd_ref.shape[0]
    for c0 in range(0, d_ff, FFN_CHUNK):
        c1 = min(c0 + FFN_CHUNK, d_ff)
        gate = _dot(xb, wgu_ref[:, c0:c1])
        up = _dot(xb, wgu_ref[:, d_ff + c0:d_ff + c1])
        h = (gate * jax.nn.sigmoid(gate) * up).astype(BF16)
        part = _dot(h, wd_ref[c0:c1, :])
        if c0 == 0:
            acc_ref[...] = part
        else:
            acc_ref[...] += part
    o_ref[...] = _layer_norm(alpha * x + acc_ref[...], g_ref[...], b_ref[...])


def _ffn(alpha, layer, x, wgu, wd, g, b):
    nt, d = x.shape
    t = ROW_TILE
    row = pl.BlockSpec((t, d), lambda i: (i, 0))
    once = lambda a: pl.BlockSpec((None,) + a.shape[1:], lambda i: (layer, 0, 0), pipeline_mode=pl.Buffered(1))
    return pl.pallas_call(
        functools.partial(_ffn_kernel, alpha),
        grid=(nt // t,),
        in_specs=[row, once(wgu), once(wd), _full(g.shape), _full(b.shape)],
        out_specs=row,
        out_shape=jax.ShapeDtypeStruct((nt, d), F32),
        scratch_shapes=[pltpu.VMEM((t, d), F32)],
        compiler_params=_cparams("parallel"),
        name="ffn",
    )(x, wgu, wd, g, b)


def _gelu_tanh(x):
    return 0.5 * x * (1.0 + jnp.tanh(0.7978845608028654 * (x + 0.044715 * (x * x * x))))


def _lru_in_kernel(x_ref, w_ref, xr_ref, gg_ref):
    z = _dot(x_ref[...].astype(BF16), w_ref[...])
    w = xr_ref.shape[1]
    xr_ref[...] = z[:, :w]
    gg_ref[...] = _gelu_tanh(z[:, w:]).astype(gg_ref.dtype)


def _lru_in(x, w):
    nt, d = x.shape
    t = ROW_TILE
    wid = w.shape[1] // 2
    row = lambda n: pl.BlockSpec((t, n), lambda i: (i, 0))
    return pl.pallas_call(
        _lru_in_kernel,
        grid=(nt // t,),
        in_specs=[row(d), _full(w.shape)],
        out_specs=[row(wid), row(wid)],
        out_shape=[jax.ShapeDtypeStruct((nt, wid), F32), jax.ShapeDtypeStruct((nt, wid), BF16)],
        compiler_params=_cparams("parallel"),
        name="lru_in",
    )(x, w)


def _lru_gates(xc, wai, ba, bi, sp):
    zg = _dot(xc.astype(BF16), wai)
    r = jax.nn.sigmoid(zg[:, :LANES] + ba)
    i = jax.nn.sigmoid(zg[:, LANES:] + bi)
    log_a = (-LRU_C) * r * sp
    a = jnp.exp(log_a)
    th = jnp.tanh(log_a)
    one_minus_a2 = -2.0 * th / (1.0 - th)
    root = jnp.where(one_minus_a2 > 0.0, one_minus_a2 * lax.rsqrt(one_minus_a2), 0.0)
    u = root * (i * xc)
    return a, u


def _softplus(y):
    return jnp.maximum(y, 0.0) + jnp.log1p(jnp.exp(-jnp.abs(y)))


def _shift_rows(v, d, fill):
    row = lax.broadcasted_iota(jnp.int32, v.shape, 0)
    return jnp.where(row >= d, pltpu.roll(v, d, axis=0), fill)


def _conv_gates(xr, tail, cw, cb, wai, ba, bi, lam):
    xp = jnp.concatenate([tail, xr], axis=0)
    xc = cb + cw[CONV_W - 1:CONV_W] * xr
    for j in range(1, CONV_W):
        xc = xc + cw[CONV_W - 1 - j:CONV_W - j] * pltpu.roll(xp, j, axis=0)[SUBLANES:]
    return _lru_gates(xc, wai, ba, bi, _softplus(-lam))


def _lru_scan_prompt_kernel(xr_ref, gg_ref, xrm_ref, ggm_ref, cw_ref, cb_ref, wai_ref, ba_ref, bi_ref, lam_ref,
                            y_ref, ym_ref, hT_ref, h_scr, tail_scr):
    c = pl.program_id(1)

    def conv_gates(xr_r, n):
        ln = slice(n * LANES, (n + 1) * LANES)
        xr = xr_r[:, ln]
        a, u = _conv_gates(xr, tail_scr[:, ln], cw_ref[:, ln], cb_ref[:, ln], wai_ref[n], ba_ref[:, ln],
                           bi_ref[:, ln], lam_ref[:, ln])
        tail_scr[:, ln] = xr[xr.shape[0] - SUBLANES:]
        return ln, a, u

    def run(xr_r, gg_r, y_r):
        n_rows = xr_r.shape[0]
        for n in range(LRU_BLOCKS):
            ln, a, u = conv_gates(xr_r, n)
            sub = lax.broadcasted_iota(jnp.int32, a.shape, 0) % SUBLANES
            for d in (1, 2, 4):
                keep = sub >= d
                u = u + a * jnp.where(keep, pltpu.roll(u, d, axis=0), 0.0)
                a = a * jnp.where(keep, pltpu.roll(a, d, axis=0), 1.0)
            h = h_scr[0:1, ln]
            pack = 2 * SUBLANES
            for k in range(n_rows // pack):
                tiles = []
                for r0 in range(k * pack, (k + 1) * pack, SUBLANES):
                    tiles.append(a[r0:r0 + SUBLANES] * h + u[r0:r0 + SUBLANES])
                    h = tiles[-1][SUBLANES - 1:]
                rows = slice(k * pack, (k + 1) * pack)
                hs = jnp.concatenate(tiles, axis=0)
                y_r[rows, ln] = (hs * gg_r[rows, ln].astype(F32)).astype(y_r.dtype)
            h_scr[:, ln] = jnp.broadcast_to(h, (SUBLANES, LANES))

    @pl.when(c == 0)
    def _():
        h_scr[...] = jnp.zeros_like(h_scr)
        tail_scr[...] = jnp.zeros_like(tail_scr)
        run(xrm_ref, ggm_ref, ym_ref)

    run(xr_ref, gg_ref, y_ref)
    hT_ref[...] = h_scr[...]


def _lru_scan_prompt(xr, gg, xrs, ggs, n_batch, seq, cw, cb, wai, ba, bi, lam):
    n_real = n_batch * seq
    w = xr.shape[1]
    t = SCAN_CHUNK
    n_c = seq // t
    real = pl.BlockSpec((t, w), lambda b, c: (b * n_c + c, 0))
    meta = pl.BlockSpec((N_META, w), lambda b, c: (b, 0))
    return pl.pallas_call(
        _lru_scan_prompt_kernel,
        grid=(n_batch, n_c),
        in_specs=[real, real, meta, meta, _full(cw.shape), _full(cb.shape), _full(wai.shape), _full(ba.shape),
                  _full(bi.shape), _full(lam.shape)],
        out_specs=[real, pl.BlockSpec((N_META, w), lambda b, c: (b, 0)), pl.BlockSpec((SUBLANES, w), lambda b, c: (b, 0))],
        out_shape=[jax.ShapeDtypeStruct((n_real, w), BF16), jax.ShapeDtypeStruct((n_batch * N_META, w), BF16),
                   jax.ShapeDtypeStruct((n_batch * SUBLANES, w), F32)],
        scratch_shapes=[pltpu.VMEM((SUBLANES, w), F32), pltpu.VMEM((SUBLANES, w), F32)],
        compiler_params=_cparams("parallel", "arbitrary"),
        name="lru_scan_prompt",
    )(xr, gg, xrs, ggs, cw, cb, wai, ba, bi, lam)


def _lru_scan_sample_kernel(xr_ref, gg_ref, buf_ref, h0_ref, cw_ref, cb_ref, wai_ref, ba_ref, bi_ref, lam_ref,
                            y_ref, hT_ref):
    n_t = xr_ref.shape[0]
    for n in range(LRU_BLOCKS):
        ln = slice(n * LANES, (n + 1) * LANES)
        rows = [buf_ref[j, :, ln] for j in range(CONV_W - 1)] + [xr_ref[t, :, ln] for t in range(n_t)]
        sp = _softplus(-lam_ref[:, ln])
        h = h0_ref[:, ln]
        for t in range(n_t):
            xc = cb_ref[:, ln]
            for j in range(CONV_W):
                xc = xc + cw_ref[j:j + 1, ln] * rows[t + j]
            a, u = _lru_gates(xc, wai_ref[n], ba_ref[:, ln], bi_ref[:, ln], sp)
            h = a * h + u
            y_ref[t, :, ln] = (h * gg_ref[t, :, ln].astype(F32)).astype(y_ref.dtype)
        hT_ref[:, ln] = h


def _lru_scan_sample(xr, gg, buf, h0, cw, cb, wai, ba, bi, lam):
    args = (xr, gg, buf, h0, cw, cb, wai, ba, bi, lam)
    return pl.pallas_call(
        _lru_scan_sample_kernel,
        grid=(1,),
        in_specs=[_full(a.shape) for a in args],
        out_specs=[_full(xr.shape), _full(h0.shape)],
        out_shape=[jax.ShapeDtypeStruct(xr.shape, BF16), jax.ShapeDtypeStruct(h0.shape, F32)],
        compiler_params=_cparams("arbitrary"),
        name="lru_scan_sample",
    )(*args)


def _rot_cols(w):
    half = w.shape[-1] // 2
    return jnp.concatenate([-w[..., half:], w[..., :half]], axis=-1)


def kernel(x_prompt, x_sample, cache_latent, cache_krope, page_table, state_lru_h, state_conv, meta_tokens,
           mla_w_in, mla_q_norm, mla_w_q_up, mla_kv_norm, mla_w_uk, mla_w_uv, mla_w_o,
           lru_w_in, lru_conv_w, lru_conv_b, lru_w_a, lru_b_a, lru_w_i, lru_b_i, lru_lambda, lru_w_o,
           ffn_w_gu, ffn_w_down, ln_g, ln_b):
    n_b, seq, d = x_prompt.shape
    n_s, n_t = x_sample.shape[:2]
    depth = ffn_w_gu.shape[0]
    d_ff = ffn_w_down.shape[1]
    alpha = (2 * depth) ** 0.25
    past_len = page_table.shape[1] * cache_latent.shape[2]
    n_real = n_b * seq
    n_meta = n_b * N_META
    n_samp = n_s * n_t
    assert seq % ATTN_BLOCK == 0 and seq % SCAN_CHUNK == 0 and n_real % ROW_TILE == 0 and PROJ_TILE == ATTN_BLOCK
    assert n_meta + n_samp <= ROW_TILE and d_ff % LANES == 0
    n_pad = ROW_TILE - n_meta - n_samp
    nt = n_real + ROW_TILE
    s0, s1 = n_meta, n_meta + n_samp

    xs = [x_prompt.reshape(n_real, d),
          jnp.concatenate([jnp.broadcast_to(meta_tokens[None], (n_b, N_META, d)).reshape(n_meta, d),
                           x_sample.transpose(1, 0, 2).reshape(n_samp, d), jnp.zeros((n_pad, d), F32)], axis=0)]
    both = lambda f, *pairs: [f(*(p[j] for p in pairs)) for j in range(2)]
    split = lambda t: [t[:n_real], t[n_real:]]

    pos = jnp.concatenate([jnp.tile(N_META + jnp.arange(seq), n_b), jnp.tile(jnp.arange(N_META), n_b),
                           jnp.repeat(past_len + jnp.arange(n_t), n_s), jnp.zeros((n_pad,), jnp.int32)])
    half = QK_ROPE // 2
    freqs = ROPE_THETA ** (-jnp.arange(half, dtype=F32) / half)
    ang = pos.astype(F32)[:, None] * freqs[None, :]
    cos2 = jnp.tile(jnp.cos(ang), (1, 2))
    sin2 = jnp.tile(jnp.sin(ang), (1, 2))
    zeros64 = jnp.zeros((nt, QK_NOPE), F32)
    cos4 = split(jnp.concatenate([zeros64, cos2, cos2], axis=1))
    sin4 = split(jnp.concatenate([zeros64, sin2, sin2], axis=1))
    t1 = split(jnp.concatenate([jnp.full((nt, QK_NOPE), ATTN_SCALE, F32), cos2 * ATTN_SCALE, sin2 * ATTN_SCALE], 1))

    even = (jnp.arange(N_HEADS) % 2 == 0)[None, :, None]
    ffn_wgu = ffn_w_gu.astype(BF16)
    ffn_wd = ffn_w_down.astype(BF16)
    cache_krope_t = cache_krope.transpose(0, 1, 3, 2)

    lat_p, kr_p, lat_s, kr_s, hp_l, cp_l, hs_l, cs_l = [], [], [], [], [], [], [], []
    for i in range(depth):
        g0, b0 = ln_g[i, 0][None], ln_b[i, 0][None]
        g1, b1 = ln_g[i, 1][None], ln_b[i, 1][None]
        if i % 2 == 0:
            a = i // 2
            w_in = mla_w_in[a]
            wr = w_in[:, Q_LORA + KV_LORA:]
            z64 = jnp.zeros((d, QK_NOPE), F32)
            w_in_ext = jnp.concatenate([w_in[:, :Q_LORA + KV_LORA], z64, wr, wr, z64, _rot_cols(wr), _rot_cols(wr)],
                                       axis=1).astype(BF16)
            wq3 = mla_w_q_up[a].reshape(Q_LORA, N_HEADS, QK_NOPE + QK_ROPE)
            wq_ext = jnp.concatenate([wq3, _rot_cols(wq3[..., QK_NOPE:])], axis=-1).reshape(Q_LORA, -1).astype(BF16)
            wuk = mla_w_uk[a]
            wuk_pad = jnp.concatenate([wuk, jnp.zeros_like(wuk)], axis=-1).reshape(KV_LORA, -1).astype(BF16)
            wuv = mla_w_uv[a]
            zv = jnp.zeros_like(wuv)
            wuv_pad3 = jnp.where(even, jnp.concatenate([wuv, zv], -1), jnp.concatenate([zv, wuv], -1)).astype(BF16)
            wuvt = wuv.reshape(KV_LORA, -1).T.astype(BF16)

            proj = lambda x_, c_, s_, t_: _mla_proj(x_, w_in_ext, mla_q_norm[a][None], mla_kv_norm[a][None],
                                                    c_, s_, t_, wq_ext, wuk_pad, wuvt)
            (ckv, krkr, _, q, k, vt), (ckv_s, krkr_s, cq_s, q_s, k_s, vt_s) = both(proj, xs, cos4, sin4, t1)
            kr, kr_s_ = (t[:, QK_NOPE:QK_NOPE + QK_ROPE] for t in (krkr, krkr_s))
            vmt = vt_s[:, 0, :, :n_meta].reshape(N_HEADS, HEAD_PAD, n_b, N_META).transpose(2, 0, 1, 3)
            o_real, o_meta_t = _attn_prompt(q, k, vt, q_s, k_s, vmt, n_b, seq)
            o_meta = o_meta_t.transpose(0, 3, 1, 2).reshape(n_meta, d).astype(BF16)

            wukt = wuk.transpose(1, 2, 0).astype(BF16)
            qlat, qrope = _sample_q(cq_s[s0:s1], wq_ext, wukt, t1[1][s0:s1])
            per_seq = lambda t: t.reshape(N_HEADS, n_t, n_s, -1).transpose(2, 1, 0, 3).reshape(n_s, n_t * N_HEADS, -1)
            new_rows = lambda t: jnp.pad(t.reshape(n_t, n_s, -1).transpose(1, 0, 2),
                                         ((0, 0), (0, SUBLANES - n_t), (0, 0)))
            o_lat = _attn_sample(a, page_table, per_seq(qlat), per_seq(qrope), new_rows(ckv_s[s0:s1]),
                                 new_rows(kr_s_[s0:s1]), cache_latent, cache_krope_t)
            o_lat = o_lat.reshape(n_s, n_t, N_HEADS, KV_LORA).transpose(2, 1, 0, 3).reshape(N_HEADS, n_samp, KV_LORA)
            o_samp = _sample_ov(o_lat.astype(BF16), wuv_pad3.transpose(1, 0, 2))

            mix = [o_real, jnp.concatenate([o_meta, o_samp, jnp.zeros((n_pad, d), BF16)], axis=0)]
            w_o = mla_w_o[a].astype(BF16)

            tok = lambda r, s: jnp.concatenate([s[:n_meta].reshape(n_b, N_META, -1), r.reshape(n_b, seq, -1)], 1)
            samp = lambda s: s[s0:s1].reshape(n_t, n_s, -1).transpose(1, 0, 2)
            lat_p.append(tok(ckv, ckv_s))
            kr_p.append(tok(kr, kr_s_))
            lat_s.append(samp(ckv_s))
            kr_s.append(samp(kr_s_))
        else:
            lb = i // 2
            w_in = lru_w_in[lb].astype(BF16)
            (xr, gg), (xr_t, gg_t) = both(lambda x_: _lru_in(x_, w_in), xs)
            wai = jnp.concatenate([lru_w_a[lb], lru_w_i[lb]], axis=-1).astype(BF16)
            lw = (lru_conv_w[lb], lru_conv_b[lb][None], wai, lru_b_a[lb][None], lru_b_i[lb][None],
                  lru_lambda[lb][None])
            y_real, y_meta, h_p = _lru_scan_prompt(xr, gg, xr_t, gg_t, n_b, seq, *lw)
            xr_s = xr_t[s0:s1].reshape(n_t, n_s, -1)
            y_samp, h_s = _lru_scan_sample(xr_s, gg_t[s0:s1].reshape(n_t, n_s, -1), state_conv[lb].transpose(1, 0, 2),
                                           state_lru_h[lb], *lw)
            mix = [y_real, jnp.concatenate([y_meta, y_samp.reshape(n_samp, -1), jnp.zeros((n_pad, d), BF16)], axis=0)]
            w_o = lru_w_o[lb].astype(BF16)

            hp_l.append(h_p.reshape(n_b, SUBLANES, -1)[:, 0])
            cp_l.append(jnp.stack([xr[(bb + 1) * seq - (CONV_W - 1):(bb + 1) * seq] for bb in range(n_b)]))
            hs_l.append(h_s)
            cs_l.append(jnp.concatenate([state_conv[lb].transpose(1, 0, 2), xr_s], axis=0)[-(CONV_W - 1):]
                        .transpose(1, 0, 2))

        xs = both(lambda x_, m_: _proj_ln(alpha, x_, m_, w_o, g0, b0), xs, mix)
        xs = both(lambda x_: _ffn(alpha, i, x_, ffn_wgu, ffn_wd, g1, b1), xs)

    y_prompt = xs[0].reshape(n_b, seq, d)
    y_sample = xs[1][s0:s1].reshape(n_t, n_s, d).transpose(1, 0, 2)
    return (y_prompt, y_sample, jnp.stack(lat_p), jnp.stack(kr_p), jnp.stack(lat_s), jnp.stack(kr_s),
            jnp.stack(hp_l), jnp.stack(cp_l), jnp.stack(hs_l), jnp.stack(cs_l))
```

```python
import functools

import jax
import jax.numpy as jnp
from jax import lax
from jax.experimental import pallas as pl
from jax.experimental.pallas import tpu as pltpu

N_META = 16
N_HEADS = 16
Q_LORA = 384
KV_LORA = 256
QK_NOPE = 64
QK_ROPE = 32
V_HEAD = 64
ROPE_THETA = 10000.0
ATTN_SCALE = (QK_NOPE + QK_ROPE) ** -0.5
LRU_BLOCKS = 8
CONV_W = 4
LRU_C = 8.0
NORM_EPS = 1e-5

LANES = 128
SUBLANES = 8
HEAD_PAD = LANES
ROW_TILE = 1024
PROJ_TILE = 512
ATTN_BLOCK = 512
HEADS_PER_STEP = 4
SCAN_CHUNK = 512
KV_CHUNK = 1024
FFN_CHUNK = 256
MIX_FFN_TILE = 512
VMEM_LIMIT = 56 * 1024 * 1024

F32 = jnp.float32
BF16 = jnp.bfloat16


def _cparams(*sem):
    return pltpu.CompilerParams(dimension_semantics=sem, vmem_limit_bytes=VMEM_LIMIT)


def _dot(a, b):
    return jnp.dot(a, b, preferred_element_type=F32)


def _dot_nt(a, b):
    return lax.dot_general(a, b, (((1,), (1,)), ((), ())), preferred_element_type=F32)


def _full(shape):
    nd = len(shape)
    return pl.BlockSpec(shape, lambda *_: (0,) * nd)


def _layer_norm(y, g, b):
    mu = jnp.mean(y, axis=-1, keepdims=True)
    d = y - mu
    var = jnp.mean(d * d, axis=-1, keepdims=True)
    return d * lax.rsqrt(var + NORM_EPS) * g + b


def _rms_norm(y, g):
    return y * lax.rsqrt(jnp.mean(y * y, axis=-1, keepdims=True) + NORM_EPS) * g


def _mla_proj_kernel(x_ref, w_in_ref, qn_ref, kvn_ref, cos_ref, sin_ref, t1_ref, wq_ref, wuk_ref, wuvt_ref,
                     ckv_ref, krkr_ref, cq_ref, q_ref, k_ref, vt_ref):
    x = x_ref[...].astype(BF16)
    z = _dot(x, w_in_ref[...])
    cq = _rms_norm(z[:, :Q_LORA], qn_ref[...])
    ckv = _rms_norm(z[:, Q_LORA:Q_LORA + KV_LORA], kvn_ref[...])
    o = Q_LORA + KV_LORA
    krkr = z[:, o:o + LANES] * cos_ref[...] + z[:, o + LANES:o + 2 * LANES] * sin_ref[...]
    ckv_ref[...] = ckv
    krkr_ref[...] = krkr
    cq_b = cq.astype(BF16)
    cq_ref[...] = cq_b
    ckv_b = ckv.astype(BF16)
    t1 = t1_ref[...]
    qz = _dot(cq_b, wq_ref[...])
    for h in range(N_HEADS):
        q_ref[h] = (qz[:, h * HEAD_PAD:(h + 1) * HEAD_PAD] * t1).astype(BF16)
    kz = _dot(ckv_b, wuk_ref[...])
    for h in range(N_HEADS):
        k_ref[h] = (kz[:, h * HEAD_PAD:(h + 1) * HEAD_PAD] + krkr).astype(BF16)
    vz = _dot_nt(wuvt_ref[...], ckv_b)
    ones = jnp.ones((V_HEAD, vz.shape[1]), BF16)
    for h in range(N_HEADS):
        val = vz[h * V_HEAD:(h + 1) * V_HEAD].astype(BF16)
        lo, hi = (val, ones) if h % 2 == 0 else (ones, val)
        vt_ref[h, 0, :V_HEAD, :] = lo
        vt_ref[h, 0, V_HEAD:, :] = hi


def _mla_proj(x, w_in, qn, kvn, cos4, sin4, t1, wq, wuk, wuvt):
    nt, d = x.shape
    t = PROJ_TILE
    row = lambda w: pl.BlockSpec((t, w), lambda i: (i, 0))
    head = pl.BlockSpec((N_HEADS, t, HEAD_PAD), lambda i: (0, i, 0))
    head_sds = jax.ShapeDtypeStruct((N_HEADS, nt, HEAD_PAD), BF16)
    return pl.pallas_call(
        _mla_proj_kernel,
        grid=(nt // t,),
        in_specs=[row(d), _full(w_in.shape), _full(qn.shape), _full(kvn.shape), row(LANES), row(LANES), row(LANES),
                  _full(wq.shape), _full(wuk.shape), _full(wuvt.shape)],
        out_specs=[row(KV_LORA), row(LANES), row(Q_LORA), head, head,
                   pl.BlockSpec((N_HEADS, 1, HEAD_PAD, t), lambda i: (0, i, 0, 0))],
        out_shape=[jax.ShapeDtypeStruct((nt, KV_LORA), F32), jax.ShapeDtypeStruct((nt, LANES), F32),
                   jax.ShapeDtypeStruct((nt, Q_LORA), BF16), head_sds, head_sds,
                   jax.ShapeDtypeStruct((N_HEADS, nt // t, HEAD_PAD, t), BF16)],
        compiler_params=_cparams("parallel"),
        name="mla_proj",
    )(x, w_in, qn, kvn, cos4, sin4, t1, wq, wuk, wuvt)


def _head_out(acc, g):
    if g % 2 == 0:
        return acc[:V_HEAD] * (1.0 / acc[V_HEAD:V_HEAD + 1])
    return acc[V_HEAD:] * (1.0 / acc[0:1])


def _attn_prompt_kernel(q_ref, k_ref, vt_ref, qm_ref, km_ref, vmt_ref, o_ref, omt_ref):
    qi = pl.program_id(2)
    blk = ATTN_BLOCK
    n_pair, tq, _ = q_ref.shape

    def scores(g, j):
        rows = pl.ds(pl.multiple_of(j * blk, blk), blk)
        return _dot_nt(k_ref[g, rows, :], q_ref[g])

    def softmax(s, m):
        m_new = jnp.maximum(m, jnp.max(s, axis=0, keepdims=True))
        return jnp.exp(s - m_new).astype(BF16), jnp.exp(m - m_new), m_new

    def values(g, j, p, alpha, acc):
        return alpha * acc + _dot(vt_ref[g, j], p)

    carry = []
    for g in range(n_pair):
        s = scores(g, qi)
        key = lax.broadcasted_iota(jnp.int32, s.shape, 0)
        qry = lax.broadcasted_iota(jnp.int32, s.shape, 1)
        s = jnp.where(key <= qry, s, -jnp.inf)
        s_m = _dot_nt(km_ref[g], q_ref[g])
        s_first = scores(g, 0)
        m = jnp.maximum(jnp.max(s, axis=0, keepdims=True), jnp.max(s_m, axis=0, keepdims=True))
        p = jnp.exp(s - m).astype(BF16)
        p_m = jnp.exp(s_m - m).astype(BF16)
        carry += [s_first, m, _dot(vt_ref[g, qi], p) + _dot(vmt_ref[g], p_m)]

    def body(j, carry):
        out = []
        for g in range(n_pair):
            s, m, acc = carry[3 * g:3 * g + 3]
            s_next = scores(g, j + 1)
            p, alpha, m = softmax(s, m)
            out += [s_next, m, values(g, j, p, alpha, acc)]
        return tuple(out)

    carry = lax.fori_loop(0, qi, body, tuple(carry))
    halves = [_head_out(carry[3 * g + 2], g) for g in range(n_pair)]
    o_ref[...] = jnp.concatenate(halves, axis=0).T.astype(o_ref.dtype)

    @pl.when(qi == 0)
    def _():
        halves = []
        for g in range(n_pair):
            s = _dot_nt(km_ref[g], qm_ref[g])
            key = lax.broadcasted_iota(jnp.int32, s.shape, 0)
            qry = lax.broadcasted_iota(jnp.int32, s.shape, 1)
            s = jnp.where(key <= qry, s, -jnp.inf)
            p = jnp.exp(s - jnp.max(s, axis=0, keepdims=True)).astype(BF16)
            halves.append(_head_out(_dot(vmt_ref[g], p), g))
        omt_ref[...] = jnp.concatenate(halves, axis=0)


def _attn_prompt(q, k, vt, qs, ks, vmt, n_batch, seq):
    n_real = n_batch * seq
    blk = ATTN_BLOCK
    n_q = seq // blk
    pair = HEADS_PER_STEP
    qspec = pl.BlockSpec((pair, blk, HEAD_PAD), lambda b, hp, qi: (hp, b * n_q + qi, 0))
    kspec = pl.BlockSpec((pair, seq, HEAD_PAD), lambda b, hp, qi: (hp, b, 0))
    vtspec = pl.BlockSpec((pair, n_q, HEAD_PAD, blk), lambda b, hp, qi: (hp, b, 0, 0))
    mspec = pl.BlockSpec((pair, N_META, HEAD_PAD), lambda b, hp, qi: (hp, b, 0))
    vmtspec = pl.BlockSpec((None, pair, HEAD_PAD, N_META), lambda b, hp, qi: (b, hp, 0, 0))
    return pl.pallas_call(
        _attn_prompt_kernel,
        grid=(n_batch, N_HEADS // pair, n_q),
        in_specs=[qspec, kspec, vtspec, mspec, mspec, vmtspec],
        out_specs=[pl.BlockSpec((blk, pair * V_HEAD), lambda b, hp, qi: (b * n_q + qi, hp)),
                   pl.BlockSpec((None, None, pair * V_HEAD, N_META), lambda b, hp, qi: (b, hp, 0, 0))],
        out_shape=[jax.ShapeDtypeStruct((n_real, N_HEADS * V_HEAD), BF16),
                   jax.ShapeDtypeStruct((n_batch, N_HEADS // pair, pair * V_HEAD, N_META), F32)],
        compiler_params=_cparams("parallel", "parallel", "arbitrary"),
        name="attn_prompt",
    )(q, k, vt, qs, ks, vmt)


def _sample_q_kernel(cq_ref, wq_ref, wukt_ref, t1_ref, qlat_ref, qrope_ref):
    qz = _dot(cq_ref[...], wq_ref[...])
    t1 = t1_ref[...]
    for h in range(N_HEADS):
        e = qz[:, h * HEAD_PAD:(h + 1) * HEAD_PAD] * t1
        qlat_ref[h] = _dot(e[:, :QK_NOPE].astype(BF16), wukt_ref[h]).astype(BF16)
        qrope_ref[h] = (e[:, QK_NOPE:QK_NOPE + QK_ROPE] + e[:, QK_NOPE + QK_ROPE:]).astype(BF16)


def _sample_q(cq, wq, wukt, t1):
    n = cq.shape[0]
    return pl.pallas_call(
        _sample_q_kernel,
        grid=(1,),
        in_specs=[_full(cq.shape), _full(wq.shape), _full(wukt.shape), _full(t1.shape)],
        out_specs=[_full((N_HEADS, n, KV_LORA)), _full((N_HEADS, n, QK_ROPE))],
        out_shape=[jax.ShapeDtypeStruct((N_HEADS, n, KV_LORA), BF16),
                   jax.ShapeDtypeStruct((N_HEADS, n, QK_ROPE), BF16)],
        compiler_params=_cparams("arbitrary"),
        name="sample_q",
    )(cq, wq, wukt, t1)


def _attn_sample_kernel(layer, n_pages, page, pt_ref, qlat_ref, qrope_ref, cnew_ref, rnew_ref, clat_hbm, ckr_hbm,
                        o_ref, cbuf, rbuf, cb16, s_scr, sem):
    b = pl.program_id(0)
    nb = pl.num_programs(0)
    slot = b % 2

    def copies(phys, slot_, p):
        rows = pl.ds(p * page, page)
        return (pltpu.make_async_copy(clat_hbm.at[layer, phys], cbuf.at[slot_, rows, :], sem.at[0, slot_]),
                pltpu.make_async_copy(ckr_hbm.at[layer, phys], rbuf.at[slot_, :, rows], sem.at[1, slot_]))

    def fetch(seq, slot_):
        for p in range(n_pages):
            for cp in copies(pt_ref[seq, p], slot_, p):
                cp.start()

    def wait(slot_):
        for p in range(n_pages):
            for cp in copies(0, slot_, p):
                cp.wait()

    @pl.when(b == 0)
    def _():
        fetch(0, 0)

    fetch(jnp.minimum(b + 1, nb - 1), 1 - slot)
    wait(slot)

    qlat = qlat_ref[0]
    qrope = qrope_ref[0]
    n_chunk = n_pages * page // KV_CHUNK
    mx = None
    for ck in range(n_chunk):
        rows = pl.ds(ck * KV_CHUNK, KV_CHUNK)
        c = cbuf[slot, rows, :].astype(BF16)
        cb16[rows, :] = c
        s = _dot_nt(qlat, c) + _dot(qrope, rbuf[slot, :, rows].astype(BF16))
        s_scr[:, rows] = s
        for i in range(KV_CHUNK // LANES):
            t = s[:, i * LANES:(i + 1) * LANES]
            mx = t if mx is None else jnp.maximum(mx, t)
    cn = cnew_ref[0].astype(BF16)
    s_n = _dot_nt(qlat, cn) + _dot_nt(qrope, rnew_ref[0].astype(BF16))
    row = lax.broadcasted_iota(jnp.int32, s_n.shape, 0)
    col = lax.broadcasted_iota(jnp.int32, s_n.shape, 1)
    s_n = jnp.where(col * N_HEADS <= row, s_n, -jnp.inf)
    m = jnp.maximum(jnp.max(mx, axis=-1, keepdims=True), jnp.max(s_n, axis=-1, keepdims=True))
    p_n = jnp.exp(s_n - m)
    l = jnp.sum(p_n, axis=-1, keepdims=True)
    acc = _dot(p_n.astype(BF16), cn)
    for ck in range(n_chunk):
        rows = pl.ds(ck * KV_CHUNK, KV_CHUNK)
        p = jnp.exp(s_scr[:, rows] - m)
        l = l + jnp.sum(p, axis=-1, keepdims=True)
        acc = acc + _dot(p.astype(BF16), cb16[rows, :])
    o_ref[0] = acc * (1.0 / l)

    @pl.when(b == nb - 1)
    def _():
        wait(1 - slot)


def _attn_sample(layer, page_table, qlat, qrope, cnew, rnew, cache_latent, cache_krope):
    n_seq, n_pages = page_table.shape
    page = cache_latent.shape[2]
    rows = qlat.shape[1]
    past = n_pages * page
    blk = lambda a: pl.BlockSpec((1,) + a.shape[1:], lambda b, pt: (b, 0, 0))
    return pl.pallas_call(
        functools.partial(_attn_sample_kernel, layer, n_pages, page),
        grid_spec=pltpu.PrefetchScalarGridSpec(
            num_scalar_prefetch=1,
            grid=(n_seq,),
            in_specs=[blk(qlat), blk(qrope), blk(cnew), blk(rnew),
                      pl.BlockSpec(memory_space=pl.ANY), pl.BlockSpec(memory_space=pl.ANY)],
            out_specs=pl.BlockSpec((1, rows, KV_LORA), lambda b, pt: (b, 0, 0)),
            scratch_shapes=[pltpu.VMEM((2, past, KV_LORA), F32), pltpu.VMEM((2, QK_ROPE, past), F32),
                            pltpu.VMEM((past, KV_LORA), BF16), pltpu.VMEM((rows, past), F32),
                            pltpu.SemaphoreType.DMA((2, 2))],
        ),
        out_shape=jax.ShapeDtypeStruct((n_seq, rows, KV_LORA), F32),
        compiler_params=_cparams("arbitrary"),
        name="attn_sample",
    )(page_table, qlat, qrope, cnew, rnew, cache_latent, cache_krope)


def _sample_ov_kernel(o_ref, wuv_ref, out_ref):
    for hp in range(N_HEADS // 2):
        a = _dot(o_ref[2 * hp], wuv_ref[2 * hp]) + _dot(o_ref[2 * hp + 1], wuv_ref[2 * hp + 1])
        out_ref[:, hp * LANES:(hp + 1) * LANES] = a.astype(out_ref.dtype)


def _sample_ov(o, wuv):
    n = o.shape[1]
    return pl.pallas_call(
        _sample_ov_kernel,
        grid=(1,),
        in_specs=[_full(o.shape), _full(wuv.shape)],
        out_specs=_full((n, N_HEADS * V_HEAD)),
        out_shape=jax.ShapeDtypeStruct((n, N_HEADS * V_HEAD), BF16),
        compiler_params=_cparams("arbitrary"),
        name="sample_ov",
    )(o, wuv)


def _mix_ffn_kernel(alpha, x_ref, y_ref, wo_ref, g0_ref, b0_ref, wgu_ref, wd_ref, g1_ref, b1_ref, o_ref,
                    x1_ref, acc_ref):
    x1_ref[...] = _layer_norm(alpha * x_ref[...] + _dot(y_ref[...], wo_ref[...]), g0_ref[...], b0_ref[...])
    xb = x1_ref[...].astype(BF16)
    d_ff = wd_ref.shape[0]
    for c0 in range(0, d_ff, FFN_CHUNK):
        c1 = min(c0 + FFN_CHUNK, d_ff)
        gate = _dot(xb, wgu_ref[:, c0:c1])
        up = _dot(xb, wgu_ref[:, d_ff + c0:d_ff + c1])
        h = (gate * jax.nn.sigmoid(gate) * up).astype(BF16)
        part = _dot(h, wd_ref[c0:c1, :])
        if c0 == 0:
            acc_ref[...] = part
        else:
            acc_ref[...] += part
    o_ref[...] = _layer_norm(alpha * x1_ref[...] + acc_ref[...], g1_ref[...], b1_ref[...])


def _mix_ffn(alpha, layer, x, y, wo, g0, b0, wgu, wd, g1, b1):
    nt, d = x.shape
    t = MIX_FFN_TILE
    row = pl.BlockSpec((t, d), lambda i: (i, 0))
    once = lambda a: pl.BlockSpec((None,) + a.shape[1:], lambda i: (layer, 0, 0), pipeline_mode=pl.Buffered(1))
    return pl.pallas_call(
        functools.partial(_mix_ffn_kernel, alpha),
        grid=(nt // t,),
        in_specs=[row, pl.BlockSpec((t, y.shape[1]), lambda i: (i, 0)), _full(wo.shape), _full(g0.shape),
                  _full(b0.shape), once(wgu), once(wd), _full(g1.shape), _full(b1.shape)],
        out_specs=row,
        out_shape=jax.ShapeDtypeStruct((nt, d), F32),
        scratch_shapes=[pltpu.VMEM((t, d), F32), pltpu.VMEM((t, d), F32)],
        compiler_params=_cparams("parallel"),
        name="mix_ffn",
    )(x, y, wo, g0, b0, wgu, wd, g1, b1)


def _gelu_tanh(x):
    return 0.5 * x * (1.0 + jnp.tanh(0.7978845608028654 * (x + 0.044715 * (x * x * x))))


def _lru_in_kernel(x_ref, w_ref, xr_ref, gg_ref):
    z = _dot(x_ref[...].astype(BF16), w_ref[...])
    w = xr_ref.shape[1]
    xr_ref[...] = z[:, :w]
    gg_ref[...] = _gelu_tanh(z[:, w:]).astype(gg_ref.dtype)


def _lru_in(x, w):
    nt, d = x.shape
    t = ROW_TILE
    wid = w.shape[1] // 2
    row = lambda n: pl.BlockSpec((t, n), lambda i: (i, 0))
    return pl.pallas_call(
        _lru_in_kernel,
        grid=(nt // t,),
        in_specs=[row(d), _full(w.shape)],
        out_specs=[row(wid), row(wid)],
        out_shape=[jax.ShapeDtypeStruct((nt, wid), F32), jax.ShapeDtypeStruct((nt, wid), BF16)],
        compiler_params=_cparams("parallel"),
        name="lru_in",
    )(x, w)


def _lru_gates(xc, wai, ba, bi, sp):
    zg = _dot(xc.astype(BF16), wai)
    r = jax.nn.sigmoid(zg[:, :LANES] + ba)
    i = jax.nn.sigmoid(zg[:, LANES:] + bi)
    log_a = (-LRU_C) * r * sp
    a = jnp.exp(log_a)
    th = jnp.tanh(log_a)
    one_minus_a2 = -2.0 * th / (1.0 - th)
    root = jnp.where(one_minus_a2 > 0.0, one_minus_a2 * lax.rsqrt(one_minus_a2), 0.0)
    u = root * (i * xc)
    return a, u


def _softplus(y):
    return jnp.maximum(y, 0.0) + jnp.log1p(jnp.exp(-jnp.abs(y)))


def _shift_rows(v, d, fill):
    row = lax.broadcasted_iota(jnp.int32, v.shape, 0)
    return jnp.where(row >= d, pltpu.roll(v, d, axis=0), fill)


def _conv_gates(xr, tail, cw, cb, wai, ba, bi, lam):
    xp = jnp.concatenate([tail, xr], axis=0)
    xc = cb + cw[CONV_W - 1:CONV_W] * xr
    for j in range(1, CONV_W):
        xc = xc + cw[CONV_W - 1 - j:CONV_W - j] * pltpu.roll(xp, j, axis=0)[SUBLANES:]
    return _lru_gates(xc, wai, ba, bi, _softplus(-lam))


def _lru_scan_prompt_kernel(xr_ref, gg_ref, xrm_ref, ggm_ref, cw_ref, cb_ref, wai_ref, ba_ref, bi_ref, lam_ref,
                            y_ref, ym_ref, hT_ref, h_scr, tail_scr):
    c = pl.program_id(1)

    def conv_gates(xr_r, n):
        ln = slice(n * LANES, (n + 1) * LANES)
        xr = xr_r[:, ln]
        a, u = _conv_gates(xr, tail_scr[:, ln], cw_ref[:, ln], cb_ref[:, ln], wai_ref[n], ba_ref[:, ln],
                           bi_ref[:, ln], lam_ref[:, ln])
        tail_scr[:, ln] = xr[xr.shape[0] - SUBLANES:]
        return ln, a, u

    def run(xr_r, gg_r, y_r):
        n_rows = xr_r.shape[0]
        for n in range(LRU_BLOCKS):
            ln, a, u = conv_gates(xr_r, n)
            sub = lax.broadcasted_iota(jnp.int32, a.shape, 0) % SUBLANES
            for d in (1, 2, 4):
                keep = sub >= d
                u = u + a * jnp.where(keep, pltpu.roll(u, d, axis=0), 0.0)
                a = a * jnp.where(keep, pltpu.roll(a, d, axis=0), 1.0)
            h = h_scr[0:1, ln]
            pack = 2 * SUBLANES
            for k in range(n_rows // pack):
                tiles = []
                for r0 in range(k * pack, (k + 1) * pack, SUBLANES):
                    tiles.append(a[r0:r0 + SUBLANES] * h + u[r0:r0 + SUBLANES])
                    h = tiles[-1][SUBLANES - 1:]
                rows = slice(k * pack, (k + 1) * pack)
                hs = jnp.concatenate(tiles, axis=0)
                y_r[rows, ln] = (hs * gg_r[rows, ln].astype(F32)).astype(y_r.dtype)
            h_scr[:, ln] = jnp.broadcast_to(h, (SUBLANES, LANES))

    @pl.when(c == 0)
    def _():
        h_scr[...] = jnp.zeros_like(h_scr)
        tail_scr[...] = jnp.zeros_like(tail_scr)
        run(xrm_ref, ggm_ref, ym_ref)

    run(xr_ref, gg_ref, y_ref)
    hT_ref[...] = h_scr[...]


def _lru_scan_prompt(xr, gg, xrs, ggs, n_batch, seq, cw, cb, wai, ba, bi, lam):
    n_real = n_batch * seq
    w = xr.shape[1]
    t = SCAN_CHUNK
    n_c = seq // t
    real = pl.BlockSpec((t, w), lambda b, c: (b * n_c + c, 0))
    meta = pl.BlockSpec((N_META, w), lambda b, c: (b, 0))
    return pl.pallas_call(
        _lru_scan_prompt_kernel,
        grid=(n_batch, n_c),
        in_specs=[real, real, meta, meta, _full(cw.shape), _full(cb.shape), _full(wai.shape), _full(ba.shape),
                  _full(bi.shape), _full(lam.shape)],
        out_specs=[real, pl.BlockSpec((N_META, w), lambda b, c: (b, 0)), pl.BlockSpec((SUBLANES, w), lambda b, c: (b, 0))],
        out_shape=[jax.ShapeDtypeStruct((n_real, w), BF16), jax.ShapeDtypeStruct((n_batch * N_META, w), BF16),
                   jax.ShapeDtypeStruct((n_batch * SUBLANES, w), F32)],
        scratch_shapes=[pltpu.VMEM((SUBLANES, w), F32), pltpu.VMEM((SUBLANES, w), F32)],
        compiler_params=_cparams("parallel", "arbitrary"),
        name="lru_scan_prompt",
    )(xr, gg, xrs, ggs, cw, cb, wai, ba, bi, lam)


def _lru_scan_sample_kernel(xr_ref, gg_ref, buf_ref, h0_ref, cw_ref, cb_ref, wai_ref, ba_ref, bi_ref, lam_ref,
                            y_ref, hT_ref):
    n_t = xr_ref.shape[0]
    for n in range(LRU_BLOCKS):
        ln = slice(n * LANES, (n + 1) * LANES)
        rows = [buf_ref[j, :, ln] for j in range(CONV_W - 1)] + [xr_ref[t, :, ln] for t in range(n_t)]
        sp = _softplus(-lam_ref[:, ln])
        h = h0_ref[:, ln]
        for t in range(n_t):
            xc = cb_ref[:, ln]
            for j in range(CONV_W):
                xc = xc + cw_ref[j:j + 1, ln] * rows[t + j]
            a, u = _lru_gates(xc, wai_ref[n], ba_ref[:, ln], bi_ref[:, ln], sp)
            h = a * h + u
            y_ref[t, :, ln] = (h * gg_ref[t, :, ln].astype(F32)).astype(y_ref.dtype)
        hT_ref[:, ln] = h


def _lru_scan_sample(xr, gg, buf, h0, cw, cb, wai, ba, bi, lam):
    args = (xr, gg, buf, h0, cw, cb, wai, ba, bi, lam)
    return pl.pallas_call(
        _lru_scan_sample_kernel,
        grid=(1,),
        in_specs=[_full(a.shape) for a in args],
        out_specs=[_full(xr.shape), _full(h0.shape)],
        out_shape=[jax.ShapeDtypeStruct(xr.shape, BF16), jax.ShapeDtypeStruct(h0.shape, F32)],
        compiler_params=_cparams("arbitrary"),
        name="lru_scan_sample",
    )(*args)


def _rot_cols(w):
    half = w.shape[-1] // 2
    return jnp.concatenate([-w[..., half:], w[..., :half]], axis=-1)


def kernel(x_prompt, x_sample, cache_latent, cache_krope, page_table, state_lru_h, state_conv, meta_tokens,
           mla_w_in, mla_q_norm, mla_w_q_up, mla_kv_norm, mla_w_uk, mla_w_uv, mla_w_o,
           lru_w_in, lru_conv_w, lru_conv_b, lru_w_a, lru_b_a, lru_w_i, lru_b_i, lru_lambda, lru_w_o,
           ffn_w_gu, ffn_w_down, ln_g, ln_b):
    n_b, seq, d = x_prompt.shape
    n_s, n_t = x_sample.shape[:2]
    depth = ffn_w_gu.shape[0]
    d_ff = ffn_w_down.shape[1]
    alpha = (2 * depth) ** 0.25
    past_len = page_table.shape[1] * cache_latent.shape[2]
    n_real = n_b * seq
    n_meta = n_b * N_META
    n_samp = n_s * n_t
    assert seq % ATTN_BLOCK == 0 and seq % SCAN_CHUNK == 0 and n_real % ROW_TILE == 0 and PROJ_TILE == ATTN_BLOCK
    assert n_meta + n_samp <= ROW_TILE and d_ff % LANES == 0
    n_pad = ROW_TILE - n_meta - n_samp
    nt = n_real + ROW_TILE
    s0, s1 = n_meta, n_meta + n_samp

    xs = [x_prompt.reshape(n_real, d),
          jnp.concatenate([jnp.broadcast_to(meta_tokens[None], (n_b, N_META, d)).reshape(n_meta, d),
                           x_sample.transpose(1, 0, 2).reshape(n_samp, d), jnp.zeros((n_pad, d), F32)], axis=0)]
    both = lambda f, *pairs: [f(*(p[j] for p in pairs)) for j in range(2)]
    split = lambda t: [t[:n_real], t[n_real:]]

    pos = jnp.concatenate([jnp.tile(N_META + jnp.arange(seq), n_b), jnp.tile(jnp.arange(N_META), n_b),
                           jnp.repeat(past_len + jnp.arange(n_t), n_s), jnp.zeros((n_pad,), jnp.int32)])
    half = QK_ROPE // 2
    freqs = ROPE_THETA ** (-jnp.arange(half, dtype=F32) / half)
    ang = pos.astype(F32)[:, None] * freqs[None, :]
    cos2 = jnp.tile(jnp.cos(ang), (1, 2))
    sin2 = jnp.tile(jnp.sin(ang), (1, 2))
    zeros64 = jnp.zeros((nt, QK_NOPE), F32)
    cos4 = split(jnp.concatenate([zeros64, cos2, cos2], axis=1))
    sin4 = split(jnp.concatenate([zeros64, sin2, sin2], axis=1))
    t1 = split(jnp.concatenate([jnp.full((nt, QK_NOPE), ATTN_SCALE, F32), cos2 * ATTN_SCALE, sin2 * ATTN_SCALE], 1))

    even = (jnp.arange(N_HEADS) % 2 == 0)[None, :, None]
    ffn_wgu = ffn_w_gu.astype(BF16)
    ffn_wd = ffn_w_down.astype(BF16)
    cache_krope_t = cache_krope.transpose(0, 1, 3, 2)

    lat_p, kr_p, lat_s, kr_s, hp_l, cp_l, hs_l, cs_l = [], [], [], [], [], [], [], []
    for i in range(depth):
        g0, b0 = ln_g[i, 0][None], ln_b[i, 0][None]
        g1, b1 = ln_g[i, 1][None], ln_b[i, 1][None]
        if i % 2 == 0:
            a = i // 2
            w_in = mla_w_in[a]
            wr = w_in[:, Q_LORA + KV_LORA:]
            z64 = jnp.zeros((d, QK_NOPE), F32)
            w_in_ext = jnp.concatenate([w_in[:, :Q_LORA + KV_LORA], z64, wr, wr, z64, _rot_cols(wr), _rot_cols(wr)],
                                       axis=1).astype(BF16)
            wq3 = mla_w_q_up[a].reshape(Q_LORA, N_HEADS, QK_NOPE + QK_ROPE)
            wq_ext = jnp.concatenate([wq3, _rot_cols(wq3[..., QK_NOPE:])], axis=-1).reshape(Q_LORA, -1).astype(BF16)
            wuk = mla_w_uk[a]
            wuk_pad = jnp.concatenate([wuk, jnp.zeros_like(wuk)], axis=-1).reshape(KV_LORA, -1).astype(BF16)
            wuv = mla_w_uv[a]
            zv = jnp.zeros_like(wuv)
            wuv_pad3 = jnp.where(even, jnp.concatenate([wuv, zv], -1), jnp.concatenate([zv, wuv], -1)).astype(BF16)
            wuvt = wuv.reshape(KV_LORA, -1).T.astype(BF16)

            proj = lambda x_, c_, s_, t_: _mla_proj(x_, w_in_ext, mla_q_norm[a][None], mla_kv_norm[a][None],
                                                    c_, s_, t_, wq_ext, wuk_pad, wuvt)
            (ckv, krkr, _, q, k, vt), (ckv_s, krkr_s, cq_s, q_s, k_s, vt_s) = both(proj, xs, cos4, sin4, t1)
            kr, kr_s_ = (t[:, QK_NOPE:QK_NOPE + QK_ROPE] for t in (krkr, krkr_s))
            vmt = vt_s[:, 0, :, :n_meta].reshape(N_HEADS, HEAD_PAD, n_b, N_META).transpose(2, 0, 1, 3)
            o_real, o_meta_t = _attn_prompt(q, k, vt, q_s, k_s, vmt, n_b, seq)
            o_meta = o_meta_t.transpose(0, 3, 1, 2).reshape(n_meta, d).astype(BF16)

            wukt = wuk.transpose(1, 2, 0).astype(BF16)
            qlat, qrope = _sample_q(cq_s[s0:s1], wq_ext, wukt, t1[1][s0:s1])
            per_seq = lambda t: t.reshape(N_HEADS, n_t, n_s, -1).transpose(2, 1, 0, 3).reshape(n_s, n_t * N_HEADS, -1)
            new_rows = lambda t: jnp.pad(t.reshape(n_t, n_s, -1).transpose(1, 0, 2),
                                         ((0, 0), (0, SUBLANES - n_t), (0, 0)))
            o_lat = _attn_sample(a, page_table, per_seq(qlat), per_seq(qrope), new_rows(ckv_s[s0:s1]),
                                 new_rows(kr_s_[s0:s1]), cache_latent, cache_krope_t)
            o_lat = o_lat.reshape(n_s, n_t, N_HEADS, KV_LORA).transpose(2, 1, 0, 3).reshape(N_HEADS, n_samp, KV_LORA)
            o_samp = _sample_ov(o_lat.astype(BF16), wuv_pad3.transpose(1, 0, 2))

            mix = [o_real, jnp.concatenate([o_meta, o_samp, jnp.zeros((n_pad, d), BF16)], axis=0)]
            w_o = mla_w_o[a].astype(BF16)

            tok = lambda r, s: jnp.concatenate([s[:n_meta].reshape(n_b, N_META, -1), r.reshape(n_b, seq, -1)], 1)
            samp = lambda s: s[s0:s1].reshape(n_t, n_s, -1).transpose(1, 0, 2)
            lat_p.append(tok(ckv, ckv_s))
            kr_p.append(tok(kr, kr_s_))
            lat_s.append(samp(ckv_s))
            kr_s.append(samp(kr_s_))
        else:
            lb = i // 2
            w_in = lru_w_in[lb].astype(BF16)
            (xr, gg), (xr_t, gg_t) = both(lambda x_: _lru_in(x_, w_in), xs)
            wai = jnp.concatenate([lru_w_a[lb], lru_w_i[lb]], axis=-1).astype(BF16)
            lw = (lru_conv_w[lb], lru_conv_b[lb][None], wai, lru_b_a[lb][None], lru_b_i[lb][None],
                  lru_lambda[lb][None])
            y_real, y_meta, h_p = _lru_scan_prompt(xr, gg, xr_t, gg_t, n_b, seq, *lw)
            xr_s = xr_t[s0:s1].reshape(n_t, n_s, -1)
            y_samp, h_s = _lru_scan_sample(xr_s, gg_t[s0:s1].reshape(n_t, n_s, -1), state_conv[lb].transpose(1, 0, 2),
                                           state_lru_h[lb], *lw)
            mix = [y_real, jnp.concatenate([y_meta, y_samp.reshape(n_samp, -1), jnp.zeros((n_pad, d), BF16)], axis=0)]
            w_o = lru_w_o[lb].astype(BF16)

            hp_l.append(h_p.reshape(n_b, SUBLANES, -1)[:, 0])
            cp_l.append(jnp.stack([xr[(bb + 1) * seq - (CONV_W - 1):(bb + 1) * seq] for bb in range(n_b)]))
            hs_l.append(h_s)
            cs_l.append(jnp.concatenate([state_conv[lb].transpose(1, 0, 2), xr_s], axis=0)[-(CONV_W - 1):]
                        .transpose(1, 0, 2))

        xs = both(lambda x_, m_: _mix_ffn(alpha, i, x_, m_, w_o, g0, b0, ffn_wgu, ffn_wd, g1, b1), xs, mix)

    y_prompt = xs[0].reshape(n_b, seq, d)
    y_sample = xs[1][s0:s1].reshape(n_t, n_s, d).transpose(1, 0, 2)
    return (y_prompt, y_sample, jnp.stack(lat_p), jnp.stack(kr_p), jnp.stack(lat_s), jnp.stack(kr_s),
            jnp.stack(hp_l), jnp.stack(cp_l), jnp.stack(hs_l), jnp.stack(cs_l))
```

```python
import functools

import jax
import jax.numpy as jnp
from jax import lax
from jax.experimental import pallas as pl
from jax.experimental.pallas import tpu as pltpu

N_META = 16
N_HEADS = 16
Q_LORA = 384
KV_LORA = 256
QK_NOPE = 64
QK_ROPE = 32
V_HEAD = 64
ROPE_THETA = 10000.0
ATTN_SCALE = (QK_NOPE + QK_ROPE) ** -0.5
LRU_BLOCKS = 8
CONV_W = 4
LRU_C = 8.0
NORM_EPS = 1e-5

LANES = 128
SUBLANES = 8
HEAD_PAD = LANES
ROW_TILE = 1024
SMALL_ALIGN = 64
PROJ_TILE = 512
ATTN_BLOCK = 512
HEADS_PER_STEP = 4
SCAN_CHUNK = 512
KV_CHUNK = 1024
FFN_CHUNK = 256
MIX_FFN_TILE = 512
VMEM_LIMIT = 56 * 1024 * 1024

F32 = jnp.float32
BF16 = jnp.bfloat16


def _cparams(*sem):
    return pltpu.CompilerParams(dimension_semantics=sem, vmem_limit_bytes=VMEM_LIMIT)


def _dot(a, b):
    return jnp.dot(a, b, preferred_element_type=F32)


def _dot_nt(a, b):
    return lax.dot_general(a, b, (((1,), (1,)), ((), ())), preferred_element_type=F32)


def _full(shape):
    nd = len(shape)
    return pl.BlockSpec(shape, lambda *_: (0,) * nd)


def _layer_norm(y, g, b):
    mu = jnp.mean(y, axis=-1, keepdims=True)
    d = y - mu
    var = jnp.mean(d * d, axis=-1, keepdims=True)
    return d * lax.rsqrt(var + NORM_EPS) * g + b


def _rms_norm(y, g):
    return y * lax.rsqrt(jnp.mean(y * y, axis=-1, keepdims=True) + NORM_EPS) * g


def _mla_proj_kernel(x_ref, w_in_ref, qn_ref, kvn_ref, cos_ref, sin_ref, t1_ref, wq_ref, wuk_ref, wuvt_ref,
                     ckv_ref, krkr_ref, cq_ref, q_ref, k_ref, vt_ref):
    x = x_ref[...].astype(BF16)
    z = _dot(x, w_in_ref[...])
    cq = _rms_norm(z[:, :Q_LORA], qn_ref[...])
    ckv = _rms_norm(z[:, Q_LORA:Q_LORA + KV_LORA], kvn_ref[...])
    o = Q_LORA + KV_LORA
    krkr = z[:, o:o + LANES] * cos_ref[...] + z[:, o + LANES:o + 2 * LANES] * sin_ref[...]
    ckv_ref[...] = ckv
    krkr_ref[...] = krkr
    cq_b = cq.astype(BF16)
    cq_ref[...] = cq_b
    ckv_b = ckv.astype(BF16)
    t1 = t1_ref[...]
    qz = _dot(cq_b, wq_ref[...])
    for h in range(N_HEADS):
        q_ref[h] = (qz[:, h * HEAD_PAD:(h + 1) * HEAD_PAD] * t1).astype(BF16)
    kz = _dot(ckv_b, wuk_ref[...])
    for h in range(N_HEADS):
        k_ref[h] = (kz[:, h * HEAD_PAD:(h + 1) * HEAD_PAD] + krkr).astype(BF16)
    vz = _dot_nt(wuvt_ref[...], ckv_b)
    ones = jnp.ones((V_HEAD, vz.shape[1]), BF16)
    for h in range(N_HEADS):
        val = vz[h * V_HEAD:(h + 1) * V_HEAD].astype(BF16)
        lo, hi = (val, ones) if h % 2 == 0 else (ones, val)
        vt_ref[h, 0, :V_HEAD, :] = lo
        vt_ref[h, 0, V_HEAD:, :] = hi


def _mla_proj(x, w_in, qn, kvn, cos4, sin4, t1, wq, wuk, wuvt):
    nt, d = x.shape
    t = PROJ_TILE if nt % PROJ_TILE == 0 else nt
    row = lambda w: pl.BlockSpec((t, w), lambda i: (i, 0))
    head = pl.BlockSpec((N_HEADS, t, HEAD_PAD), lambda i: (0, i, 0))
    head_sds = jax.ShapeDtypeStruct((N_HEADS, nt, HEAD_PAD), BF16)
    return pl.pallas_call(
        _mla_proj_kernel,
        grid=(nt // t,),
        in_specs=[row(d), _full(w_in.shape), _full(qn.shape), _full(kvn.shape), row(LANES), row(LANES), row(LANES),
                  _full(wq.shape), _full(wuk.shape), _full(wuvt.shape)],
        out_specs=[row(KV_LORA), row(LANES), row(Q_LORA), head, head,
                   pl.BlockSpec((N_HEADS, 1, HEAD_PAD, t), lambda i: (0, i, 0, 0))],
        out_shape=[jax.ShapeDtypeStruct((nt, KV_LORA), F32), jax.ShapeDtypeStruct((nt, LANES), F32),
                   jax.ShapeDtypeStruct((nt, Q_LORA), BF16), head_sds, head_sds,
                   jax.ShapeDtypeStruct((N_HEADS, nt // t, HEAD_PAD, t), BF16)],
        compiler_params=_cparams("parallel"),
        name="mla_proj",
    )(x, w_in, qn, kvn, cos4, sin4, t1, wq, wuk, wuvt)


def _head_out(acc, g):
    if g % 2 == 0:
        return acc[:V_HEAD] * (1.0 / acc[V_HEAD:V_HEAD + 1])
    return acc[V_HEAD:] * (1.0 / acc[0:1])


def _attn_prompt_kernel(q_ref, k_ref, vt_ref, qm_ref, km_ref, vmt_ref, o_ref, omt_ref):
    qi = pl.program_id(2)
    blk = ATTN_BLOCK
    n_pair, tq, _ = q_ref.shape

    def scores(g, j):
        rows = pl.ds(pl.multiple_of(j * blk, blk), blk)
        return _dot_nt(k_ref[g, rows, :], q_ref[g])

    def softmax(s, m):
        m_new = jnp.maximum(m, jnp.max(s, axis=0, keepdims=True))
        return jnp.exp(s - m_new).astype(BF16), jnp.exp(m - m_new), m_new

    def values(g, j, p, alpha, acc):
        return alpha * acc + _dot(vt_ref[g, j], p)

    carry = []
    for g in range(n_pair):
        s = scores(g, qi)
        key = lax.broadcasted_iota(jnp.int32, s.shape, 0)
        qry = lax.broadcasted_iota(jnp.int32, s.shape, 1)
        s = jnp.where(key <= qry, s, -jnp.inf)
        s_m = _dot_nt(km_ref[g], q_ref[g])
        s_first = scores(g, 0)
        m = jnp.maximum(jnp.max(s, axis=0, keepdims=True), jnp.max(s_m, axis=0, keepdims=True))
        p = jnp.exp(s - m).astype(BF16)
        p_m = jnp.exp(s_m - m).astype(BF16)
        carry += [s_first, m, _dot(vt_ref[g, qi], p) + _dot(vmt_ref[g], p_m)]

    def body(j, carry):
        out = []
        for g in range(n_pair):
            s, m, acc = carry[3 * g:3 * g + 3]
            s_next = scores(g, j + 1)
            p, alpha, m = softmax(s, m)
            out += [s_next, m, values(g, j, p, alpha, acc)]
        return tuple(out)

    carry = lax.fori_loop(0, qi, body, tuple(carry))
    halves = [_head_out(carry[3 * g + 2], g) for g in range(n_pair)]
    o_ref[...] = jnp.concatenate(halves, axis=0).T.astype(o_ref.dtype)

    @pl.when(qi == 0)
    def _():
        halves = []
        for g in range(n_pair):
            s = _dot_nt(km_ref[g], qm_ref[g])
            key = lax.broadcasted_iota(jnp.int32, s.shape, 0)
            qry = lax.broadcasted_iota(jnp.int32, s.shape, 1)
            s = jnp.where(key <= qry, s, -jnp.inf)
            p = jnp.exp(s - jnp.max(s, axis=0, keepdims=True)).astype(BF16)
            halves.append(_head_out(_dot(vmt_ref[g], p), g))
        omt_ref[...] = jnp.concatenate(halves, axis=0)


def _attn_prompt(q, k, vt, qs, ks, vmt, n_batch, seq):
    n_real = n_batch * seq
    blk = ATTN_BLOCK
    n_q = seq // blk
    pair = HEADS_PER_STEP
    qspec = pl.BlockSpec((pair, blk, HEAD_PAD), lambda b, hp, qi: (hp, b * n_q + qi, 0))
    kspec = pl.BlockSpec((pair, seq, HEAD_PAD), lambda b, hp, qi: (hp, b, 0))
    vtspec = pl.BlockSpec((pair, n_q, HEAD_PAD, blk), lambda b, hp, qi: (hp, b, 0, 0))
    mspec = pl.BlockSpec((pair, N_META, HEAD_PAD), lambda b, hp, qi: (hp, b, 0))
    vmtspec = pl.BlockSpec((None, pair, HEAD_PAD, N_META), lambda b, hp, qi: (b, hp, 0, 0))
    return pl.pallas_call(
        _attn_prompt_kernel,
        grid=(n_batch, N_HEADS // pair, n_q),
        in_specs=[qspec, kspec, vtspec, mspec, mspec, vmtspec],
        out_specs=[pl.BlockSpec((blk, pair * V_HEAD), lambda b, hp, qi: (b * n_q + qi, hp)),
                   pl.BlockSpec((None, None, pair * V_HEAD, N_META), lambda b, hp, qi: (b, hp, 0, 0))],
        out_shape=[jax.ShapeDtypeStruct((n_real, N_HEADS * V_HEAD), BF16),
                   jax.ShapeDtypeStruct((n_batch, N_HEADS // pair, pair * V_HEAD, N_META), F32)],
        compiler_params=_cparams("parallel", "parallel", "arbitrary"),
        name="attn_prompt",
    )(q, k, vt, qs, ks, vmt)


def _sample_q_kernel(cq_ref, wq_ref, wukt_ref, t1_ref, qlat_ref, qrope_ref):
    qz = _dot(cq_ref[...], wq_ref[...])
    t1 = t1_ref[...]
    for h in range(N_HEADS):
        e = qz[:, h * HEAD_PAD:(h + 1) * HEAD_PAD] * t1
        qlat_ref[h] = _dot(e[:, :QK_NOPE].astype(BF16), wukt_ref[h]).astype(BF16)
        qrope_ref[h] = (e[:, QK_NOPE:QK_NOPE + QK_ROPE] + e[:, QK_NOPE + QK_ROPE:]).astype(BF16)


def _sample_q(cq, wq, wukt, t1):
    n = cq.shape[0]
    return pl.pallas_call(
        _sample_q_kernel,
        grid=(1,),
        in_specs=[_full(cq.shape), _full(wq.shape), _full(wukt.shape), _full(t1.shape)],
        out_specs=[_full((N_HEADS, n, KV_LORA)), _full((N_HEADS, n, QK_ROPE))],
        out_shape=[jax.ShapeDtypeStruct((N_HEADS, n, KV_LORA), BF16),
                   jax.ShapeDtypeStruct((N_HEADS, n, QK_ROPE), BF16)],
        compiler_params=_cparams("arbitrary"),
        name="sample_q",
    )(cq, wq, wukt, t1)


def _attn_sample_kernel(layer, n_pages, page, pt_ref, qlat_ref, qrope_ref, cnew_ref, rnew_ref, clat_hbm, ckr_hbm,
                        o_ref, cbuf, rbuf, cb16, s_scr, sem):
    b = pl.program_id(0)
    nb = pl.num_programs(0)
    slot = b % 2

    def copies(phys, slot_, p):
        rows = pl.ds(p * page, page)
        return (pltpu.make_async_copy(clat_hbm.at[layer, phys], cbuf.at[slot_, rows, :], sem.at[0, slot_]),
                pltpu.make_async_copy(ckr_hbm.at[layer, phys], rbuf.at[slot_, :, rows], sem.at[1, slot_]))

    def fetch(seq, slot_):
        for p in range(n_pages):
            for cp in copies(pt_ref[seq, p], slot_, p):
                cp.start()

    def wait(slot_):
        for p in range(n_pages):
            for cp in copies(0, slot_, p):
                cp.wait()

    @pl.when(b == 0)
    def _():
        fetch(0, 0)

    fetch(jnp.minimum(b + 1, nb - 1), 1 - slot)
    wait(slot)

    qlat = qlat_ref[0]
    qrope = qrope_ref[0]
    n_chunk = n_pages * page // KV_CHUNK
    mx = None
    for ck in range(n_chunk):
        rows = pl.ds(ck * KV_CHUNK, KV_CHUNK)
        c = cbuf[slot, rows, :].astype(BF16)
        cb16[rows, :] = c
        s = _dot_nt(qlat, c) + _dot(qrope, rbuf[slot, :, rows].astype(BF16))
        s_scr[:, rows] = s
        for i in range(KV_CHUNK // LANES):
            t = s[:, i * LANES:(i + 1) * LANES]
            mx = t if mx is None else jnp.maximum(mx, t)
    cn = cnew_ref[0].astype(BF16)
    s_n = _dot_nt(qlat, cn) + _dot_nt(qrope, rnew_ref[0].astype(BF16))
    row = lax.broadcasted_iota(jnp.int32, s_n.shape, 0)
    col = lax.broadcasted_iota(jnp.int32, s_n.shape, 1)
    s_n = jnp.where(col * N_HEADS <= row, s_n, -jnp.inf)
    m = jnp.maximum(jnp.max(mx, axis=-1, keepdims=True), jnp.max(s_n, axis=-1, keepdims=True))
    p_n = jnp.exp(s_n - m)
    l = jnp.sum(p_n, axis=-1, keepdims=True)
    acc = _dot(p_n.astype(BF16), cn)
    for ck in range(n_chunk):
        rows = pl.ds(ck * KV_CHUNK, KV_CHUNK)
        p = jnp.exp(s_scr[:, rows] - m)
        l = l + jnp.sum(p, axis=-1, keepdims=True)
        acc = acc + _dot(p.astype(BF16), cb16[rows, :])
    o_ref[0] = acc * (1.0 / l)

    @pl.when(b == nb - 1)
    def _():
        wait(1 - slot)


def _attn_sample(layer, page_table, qlat, qrope, cnew, rnew, cache_latent, cache_krope):
    n_seq, n_pages = page_table.shape
    page = cache_latent.shape[2]
    rows = qlat.shape[1]
    past = n_pages * page
    blk = lambda a: pl.BlockSpec((1,) + a.shape[1:], lambda b, pt: (b, 0, 0))
    return pl.pallas_call(
        functools.partial(_attn_sample_kernel, layer, n_pages, page),
        grid_spec=pltpu.PrefetchScalarGridSpec(
            num_scalar_prefetch=1,
            grid=(n_seq,),
            in_specs=[blk(qlat), blk(qrope), blk(cnew), blk(rnew),
                      pl.BlockSpec(memory_space=pl.ANY), pl.BlockSpec(memory_space=pl.ANY)],
            out_specs=pl.BlockSpec((1, rows, KV_LORA), lambda b, pt: (b, 0, 0)),
            scratch_shapes=[pltpu.VMEM((2, past, KV_LORA), F32), pltpu.VMEM((2, QK_ROPE, past), F32),
                            pltpu.VMEM((past, KV_LORA), BF16), pltpu.VMEM((rows, past), F32),
                            pltpu.SemaphoreType.DMA((2, 2))],
        ),
        out_shape=jax.ShapeDtypeStruct((n_seq, rows, KV_LORA), F32),
        compiler_params=_cparams("arbitrary"),
        name="attn_sample",
    )(page_table, qlat, qrope, cnew, rnew, cache_latent, cache_krope)


def _sample_ov_kernel(o_ref, wuv_ref, out_ref):
    for hp in range(N_HEADS // 2):
        a = _dot(o_ref[2 * hp], wuv_ref[2 * hp]) + _dot(o_ref[2 * hp + 1], wuv_ref[2 * hp + 1])
        out_ref[:, hp * LANES:(hp + 1) * LANES] = a.astype(out_ref.dtype)


def _sample_ov(o, wuv):
    n = o.shape[1]
    return pl.pallas_call(
        _sample_ov_kernel,
        grid=(1,),
        in_specs=[_full(o.shape), _full(wuv.shape)],
        out_specs=_full((n, N_HEADS * V_HEAD)),
        out_shape=jax.ShapeDtypeStruct((n, N_HEADS * V_HEAD), BF16),
        compiler_params=_cparams("arbitrary"),
        name="sample_ov",
    )(o, wuv)


def _mix_ffn_kernel(alpha, x_ref, y_ref, wo_ref, g0_ref, b0_ref, wgu_ref, wd_ref, g1_ref, b1_ref, o_ref,
                    x1_ref, acc_ref):
    x1_ref[...] = _layer_norm(alpha * x_ref[...] + _dot(y_ref[...], wo_ref[...]), g0_ref[...], b0_ref[...])
    xb = x1_ref[...].astype(BF16)
    d_ff = wd_ref.shape[0]
    for c0 in range(0, d_ff, FFN_CHUNK):
        c1 = min(c0 + FFN_CHUNK, d_ff)
        gate = _dot(xb, wgu_ref[:, c0:c1])
        up = _dot(xb, wgu_ref[:, d_ff + c0:d_ff + c1])
        h = (gate * jax.nn.sigmoid(gate) * up).astype(BF16)
        part = _dot(h, wd_ref[c0:c1, :])
        if c0 == 0:
            acc_ref[...] = part
        else:
            acc_ref[...] += part
    o_ref[...] = _layer_norm(alpha * x1_ref[...] + acc_ref[...], g1_ref[...], b1_ref[...])


def _mix_ffn(alpha, layer, x, y, wo, g0, b0, wgu, wd, g1, b1):
    nt, d = x.shape
    t = MIX_FFN_TILE if nt % MIX_FFN_TILE == 0 else nt
    row = pl.BlockSpec((t, d), lambda i: (i, 0))
    once = lambda a: pl.BlockSpec((None,) + a.shape[1:], lambda i: (layer, 0, 0), pipeline_mode=pl.Buffered(1))
    return pl.pallas_call(
        functools.partial(_mix_ffn_kernel, alpha),
        grid=(nt // t,),
        in_specs=[row, pl.BlockSpec((t, y.shape[1]), lambda i: (i, 0)), _full(wo.shape), _full(g0.shape),
                  _full(b0.shape), once(wgu), once(wd), _full(g1.shape), _full(b1.shape)],
        out_specs=row,
        out_shape=jax.ShapeDtypeStruct((nt, d), F32),
        scratch_shapes=[pltpu.VMEM((t, d), F32), pltpu.VMEM((t, d), F32)],
        compiler_params=_cparams("parallel"),
        name="mix_ffn",
    )(x, y, wo, g0, b0, wgu, wd, g1, b1)


def _gelu_tanh(x):
    return 0.5 * x * (1.0 + jnp.tanh(0.7978845608028654 * (x + 0.044715 * (x * x * x))))


def _lru_in_kernel(x_ref, w_ref, xr_ref, gg_ref):
    z = _dot(x_ref[...].astype(BF16), w_ref[...])
    w = xr_ref.shape[1]
    xr_ref[...] = z[:, :w]
    gg_ref[...] = _gelu_tanh(z[:, w:]).astype(gg_ref.dtype)


def _lru_in(x, w):
    nt, d = x.shape
    t = ROW_TILE if nt % ROW_TILE == 0 else nt
    wid = w.shape[1] // 2
    row = lambda n: pl.BlockSpec((t, n), lambda i: (i, 0))
    return pl.pallas_call(
        _lru_in_kernel,
        grid=(nt // t,),
        in_specs=[row(d), _full(w.shape)],
        out_specs=[row(wid), row(wid)],
        out_shape=[jax.ShapeDtypeStruct((nt, wid), F32), jax.ShapeDtypeStruct((nt, wid), BF16)],
        compiler_params=_cparams("parallel"),
        name="lru_in",
    )(x, w)


def _lru_gates(xc, wai, ba, bi, sp):
    zg = _dot(xc.astype(BF16), wai)
    r = jax.nn.sigmoid(zg[:, :LANES] + ba)
    i = jax.nn.sigmoid(zg[:, LANES:] + bi)
    log_a = (-LRU_C) * r * sp
    a = jnp.exp(log_a)
    th = jnp.tanh(log_a)
    one_minus_a2 = -2.0 * th / (1.0 - th)
    root = jnp.where(one_minus_a2 > 0.0, one_minus_a2 * lax.rsqrt(one_minus_a2), 0.0)
    u = root * (i * xc)
    return a, u


def _softplus(y):
    return jnp.maximum(y, 0.0) + jnp.log1p(jnp.exp(-jnp.abs(y)))


def _shift_rows(v, d, fill):
    row = lax.broadcasted_iota(jnp.int32, v.shape, 0)
    return jnp.where(row >= d, pltpu.roll(v, d, axis=0), fill)


def _conv_gates(xr, tail, cw, cb, wai, ba, bi, lam):
    xp = jnp.concatenate([tail, xr], axis=0)
    xc = cb + cw[CONV_W - 1:CONV_W] * xr
    for j in range(1, CONV_W):
        xc = xc + cw[CONV_W - 1 - j:CONV_W - j] * pltpu.roll(xp, j, axis=0)[SUBLANES:]
    return _lru_gates(xc, wai, ba, bi, _softplus(-lam))


def _lru_scan_prompt_kernel(xr_ref, gg_ref, xrm_ref, ggm_ref, cw_ref, cb_ref, wai_ref, ba_ref, bi_ref, lam_ref,
                            y_ref, ym_ref, hT_ref, h_scr, tail_scr):
    c = pl.program_id(1)

    def conv_gates(xr_r, n):
        ln = slice(n * LANES, (n + 1) * LANES)
        xr = xr_r[:, ln]
        a, u = _conv_gates(xr, tail_scr[:, ln], cw_ref[:, ln], cb_ref[:, ln], wai_ref[n], ba_ref[:, ln],
                           bi_ref[:, ln], lam_ref[:, ln])
        tail_scr[:, ln] = xr[xr.shape[0] - SUBLANES:]
        return ln, a, u

    def run(xr_r, gg_r, y_r):
        n_rows = xr_r.shape[0]
        for n in range(LRU_BLOCKS):
            ln, a, u = conv_gates(xr_r, n)
            sub = lax.broadcasted_iota(jnp.int32, a.shape, 0) % SUBLANES
            for d in (1, 2, 4):
                keep = sub >= d
                u = u + a * jnp.where(keep, pltpu.roll(u, d, axis=0), 0.0)
                a = a * jnp.where(keep, pltpu.roll(a, d, axis=0), 1.0)
            h = h_scr[0:1, ln]
            pack = 2 * SUBLANES
            for k in range(n_rows // pack):
                tiles = []
                for r0 in range(k * pack, (k + 1) * pack, SUBLANES):
                    tiles.append(a[r0:r0 + SUBLANES] * h + u[r0:r0 + SUBLANES])
                    h = tiles[-1][SUBLANES - 1:]
                rows = slice(k * pack, (k + 1) * pack)
                hs = jnp.concatenate(tiles, axis=0)
                y_r[rows, ln] = (hs * gg_r[rows, ln].astype(F32)).astype(y_r.dtype)
            h_scr[:, ln] = jnp.broadcast_to(h, (SUBLANES, LANES))

    @pl.when(c == 0)
    def _():
        h_scr[...] = jnp.zeros_like(h_scr)
        tail_scr[...] = jnp.zeros_like(tail_scr)
        run(xrm_ref, ggm_ref, ym_ref)

    run(xr_ref, gg_ref, y_ref)
    hT_ref[...] = h_scr[...]


def _lru_scan_prompt(xr, gg, xrs, ggs, n_batch, seq, cw, cb, wai, ba, bi, lam):
    n_real = n_batch * seq
    w = xr.shape[1]
    t = SCAN_CHUNK
    n_c = seq // t
    real = pl.BlockSpec((t, w), lambda b, c: (b * n_c + c, 0))
    meta = pl.BlockSpec((N_META, w), lambda b, c: (b, 0))
    return pl.pallas_call(
        _lru_scan_prompt_kernel,
        grid=(n_batch, n_c),
        in_specs=[real, real, meta, meta, _full(cw.shape), _full(cb.shape), _full(wai.shape), _full(ba.shape),
                  _full(bi.shape), _full(lam.shape)],
        out_specs=[real, pl.BlockSpec((N_META, w), lambda b, c: (b, 0)), pl.BlockSpec((SUBLANES, w), lambda b, c: (b, 0))],
        out_shape=[jax.ShapeDtypeStruct((n_real, w), BF16), jax.ShapeDtypeStruct((n_batch * N_META, w), BF16),
                   jax.ShapeDtypeStruct((n_batch * SUBLANES, w), F32)],
        scratch_shapes=[pltpu.VMEM((SUBLANES, w), F32), pltpu.VMEM((SUBLANES, w), F32)],
        compiler_params=_cparams("parallel", "arbitrary"),
        name="lru_scan_prompt",
    )(xr, gg, xrs, ggs, cw, cb, wai, ba, bi, lam)


def _lru_scan_sample_kernel(xr_ref, gg_ref, buf_ref, h0_ref, cw_ref, cb_ref, wai_ref, ba_ref, bi_ref, lam_ref,
                            y_ref, hT_ref):
    n_t = xr_ref.shape[0]
    for n in range(LRU_BLOCKS):
        ln = slice(n * LANES, (n + 1) * LANES)
        rows = [buf_ref[j, :, ln] for j in range(CONV_W - 1)] + [xr_ref[t, :, ln] for t in range(n_t)]
        sp = _softplus(-lam_ref[:, ln])
        h = h0_ref[:, ln]
        for t in range(n_t):
            xc = cb_ref[:, ln]
            for j in range(CONV_W):
                xc = xc + cw_ref[j:j + 1, ln] * rows[t + j]
            a, u = _lru_gates(xc, wai_ref[n], ba_ref[:, ln], bi_ref[:, ln], sp)
            h = a * h + u
            y_ref[t, :, ln] = (h * gg_ref[t, :, ln].astype(F32)).astype(y_ref.dtype)
        hT_ref[:, ln] = h


def _lru_scan_sample(xr, gg, buf, h0, cw, cb, wai, ba, bi, lam):
    args = (xr, gg, buf, h0, cw, cb, wai, ba, bi, lam)
    return pl.pallas_call(
        _lru_scan_sample_kernel,
        grid=(1,),
        in_specs=[_full(a.shape) for a in args],
        out_specs=[_full(xr.shape), _full(h0.shape)],
        out_shape=[jax.ShapeDtypeStruct(xr.shape, BF16), jax.ShapeDtypeStruct(h0.shape, F32)],
        compiler_params=_cparams("arbitrary"),
        name="lru_scan_sample",
    )(*args)


def _rot_cols(w):
    half = w.shape[-1] // 2
    return jnp.concatenate([-w[..., half:], w[..., :half]], axis=-1)


def kernel(x_prompt, x_sample, cache_latent, cache_krope, page_table, state_lru_h, state_conv, meta_tokens,
           mla_w_in, mla_q_norm, mla_w_q_up, mla_kv_norm, mla_w_uk, mla_w_uv, mla_w_o,
           lru_w_in, lru_conv_w, lru_conv_b, lru_w_a, lru_b_a, lru_w_i, lru_b_i, lru_lambda, lru_w_o,
           ffn_w_gu, ffn_w_down, ln_g, ln_b):
    n_b, seq, d = x_prompt.shape
    n_s, n_t = x_sample.shape[:2]
    depth = ffn_w_gu.shape[0]
    d_ff = ffn_w_down.shape[1]
    alpha = (2 * depth) ** 0.25
    past_len = page_table.shape[1] * cache_latent.shape[2]
    n_real = n_b * seq
    n_meta = n_b * N_META
    n_samp = n_s * n_t
    assert seq % ATTN_BLOCK == 0 and seq % SCAN_CHUNK == 0 and n_real % ROW_TILE == 0 and PROJ_TILE == ATTN_BLOCK
    assert d_ff % LANES == 0
    n_small = -(-(n_meta + n_samp) // SMALL_ALIGN) * SMALL_ALIGN
    n_pad = n_small - n_meta - n_samp
    nt = n_real + n_small
    s0, s1 = n_meta, n_meta + n_samp

    xs = [x_prompt.reshape(n_real, d),
          jnp.concatenate([jnp.broadcast_to(meta_tokens[None], (n_b, N_META, d)).reshape(n_meta, d),
                           x_sample.transpose(1, 0, 2).reshape(n_samp, d), jnp.zeros((n_pad, d), F32)], axis=0)]
    both = lambda f, *pairs: [f(*(p[j] for p in pairs)) for j in range(2)]
    split = lambda t: [t[:n_real], t[n_real:]]

    pos = jnp.concatenate([jnp.tile(N_META + jnp.arange(seq), n_b), jnp.tile(jnp.arange(N_META), n_b),
                           jnp.repeat(past_len + jnp.arange(n_t), n_s), jnp.zeros((n_pad,), jnp.int32)])
    half = QK_ROPE // 2
    freqs = ROPE_THETA ** (-jnp.arange(half, dtype=F32) / half)
    ang = pos.astype(F32)[:, None] * freqs[None, :]
    cos2 = jnp.tile(jnp.cos(ang), (1, 2))
    sin2 = jnp.tile(jnp.sin(ang), (1, 2))
    zeros64 = jnp.zeros((nt, QK_NOPE), F32)
    cos4 = split(jnp.concatenate([zeros64, cos2, cos2], axis=1))
    sin4 = split(jnp.concatenate([zeros64, sin2, sin2], axis=1))
    t1 = split(jnp.concatenate([jnp.full((nt, QK_NOPE), ATTN_SCALE, F32), cos2 * ATTN_SCALE, sin2 * ATTN_SCALE], 1))

    even = (jnp.arange(N_HEADS) % 2 == 0)[None, :, None]
    ffn_wgu = ffn_w_gu.astype(BF16)
    ffn_wd = ffn_w_down.astype(BF16)
    cache_krope_t = cache_krope.transpose(0, 1, 3, 2)

    lat_p, kr_p, lat_s, kr_s, hp_l, cp_l, hs_l, cs_l = [], [], [], [], [], [], [], []
    for i in range(depth):
        g0, b0 = ln_g[i, 0][None], ln_b[i, 0][None]
        g1, b1 = ln_g[i, 1][None], ln_b[i, 1][None]
        if i % 2 == 0:
            a = i // 2
            w_in = mla_w_in[a]
            wr = w_in[:, Q_LORA + KV_LORA:]
            z64 = jnp.zeros((d, QK_NOPE), F32)
            w_in_ext = jnp.concatenate([w_in[:, :Q_LORA + KV_LORA], z64, wr, wr, z64, _rot_cols(wr), _rot_cols(wr)],
                                       axis=1).astype(BF16)
            wq3 = mla_w_q_up[a].reshape(Q_LORA, N_HEADS, QK_NOPE + QK_ROPE)
            wq_ext = jnp.concatenate([wq3, _rot_cols(wq3[..., QK_NOPE:])], axis=-1).reshape(Q_LORA, -1).astype(BF16)
            wuk = mla_w_uk[a]
            wuk_pad = jnp.concatenate([wuk, jnp.zeros_like(wuk)], axis=-1).reshape(KV_LORA, -1).astype(BF16)
            wuv = mla_w_uv[a]
            zv = jnp.zeros_like(wuv)
            wuv_pad3 = jnp.where(even, jnp.concatenate([wuv, zv], -1), jnp.concatenate([zv, wuv], -1)).astype(BF16)
            wuvt = wuv.reshape(KV_LORA, -1).T.astype(BF16)

            proj = lambda x_, c_, s_, t_: _mla_proj(x_, w_in_ext, mla_q_norm[a][None], mla_kv_norm[a][None],
                                                    c_, s_, t_, wq_ext, wuk_pad, wuvt)
            (ckv, krkr, _, q, k, vt), (ckv_s, krkr_s, cq_s, q_s, k_s, vt_s) = both(proj, xs, cos4, sin4, t1)
            kr, kr_s_ = (t[:, QK_NOPE:QK_NOPE + QK_ROPE] for t in (krkr, krkr_s))
            vmt = vt_s[:, 0, :, :n_meta].reshape(N_HEADS, HEAD_PAD, n_b, N_META).transpose(2, 0, 1, 3)
            o_real, o_meta_t = _attn_prompt(q, k, vt, q_s, k_s, vmt, n_b, seq)
            o_meta = o_meta_t.transpose(0, 3, 1, 2).reshape(n_meta, d).astype(BF16)

            wukt = wuk.transpose(1, 2, 0).astype(BF16)
            qlat, qrope = _sample_q(cq_s[s0:s1], wq_ext, wukt, t1[1][s0:s1])
            per_seq = lambda t: t.reshape(N_HEADS, n_t, n_s, -1).transpose(2, 1, 0, 3).reshape(n_s, n_t * N_HEADS, -1)
            new_rows = lambda t: jnp.pad(t.reshape(n_t, n_s, -1).transpose(1, 0, 2),
                                         ((0, 0), (0, SUBLANES - n_t), (0, 0)))
            o_lat = _attn_sample(a, page_table, per_seq(qlat), per_seq(qrope), new_rows(ckv_s[s0:s1]),
                                 new_rows(kr_s_[s0:s1]), cache_latent, cache_krope_t)
            o_lat = o_lat.reshape(n_s, n_t, N_HEADS, KV_LORA).transpose(2, 1, 0, 3).reshape(N_HEADS, n_samp, KV_LORA)
            o_samp = _sample_ov(o_lat.astype(BF16), wuv_pad3.transpose(1, 0, 2))

            mix = [o_real, jnp.concatenate([o_meta, o_samp, jnp.zeros((n_pad, d), BF16)], axis=0)]
            w_o = mla_w_o[a].astype(BF16)

            tok = lambda r, s: jnp.concatenate([s[:n_meta].reshape(n_b, N_META, -1), r.reshape(n_b, seq, -1)], 1)
            samp = lambda s: s[s0:s1].reshape(n_t, n_s, -1).transpose(1, 0, 2)
            lat_p.append(tok(ckv, ckv_s))
            kr_p.append(tok(kr, kr_s_))
            lat_s.append(samp(ckv_s))
            kr_s.append(samp(kr_s_))
        else:
            lb = i // 2
            w_in = lru_w_in[lb].astype(BF16)
            (xr, gg), (xr_t, gg_t) = both(lambda x_: _lru_in(x_, w_in), xs)
            wai = jnp.concatenate([lru_w_a[lb], lru_w_i[lb]], axis=-1).astype(BF16)
            lw = (lru_conv_w[lb], lru_conv_b[lb][None], wai, lru_b_a[lb][None], lru_b_i[lb][None],
                  lru_lambda[lb][None])
            y_real, y_meta, h_p = _lru_scan_prompt(xr, gg, xr_t, gg_t, n_b, seq, *lw)
            xr_s = xr_t[s0:s1].reshape(n_t, n_s, -1)
            y_samp, h_s = _lru_scan_sample(xr_s, gg_t[s0:s1].reshape(n_t, n_s, -1), state_conv[lb].transpose(1, 0, 2),
                                           state_lru_h[lb], *lw)
            mix = [y_real, jnp.concatenate([y_meta, y_samp.reshape(n_samp, -1), jnp.zeros((n_pad, d), BF16)], axis=0)]
            w_o = lru_w_o[lb].astype(BF16)

            hp_l.append(h_p.reshape(n_b, SUBLANES, -1)[:, 0])
            cp_l.append(jnp.stack([xr[(bb + 1) * seq - (CONV_W - 1):(bb + 1) * seq] for bb in range(n_b)]))
            hs_l.append(h_s)
            cs_l.append(jnp.concatenate([state_conv[lb].transpose(1, 0, 2), xr_s], axis=0)[-(CONV_W - 1):]
                        .transpose(1, 0, 2))

        xs = both(lambda x_, m_: _mix_ffn(alpha, i, x_, m_, w_o, g0, b0, ffn_wgu, ffn_wd, g1, b1), xs, mix)

    y_prompt = xs[0].reshape(n_b, seq, d)
    y_sample = xs[1][s0:s1].reshape(n_t, n_s, d).transpose(1, 0, 2)
    return (y_prompt, y_sample, jnp.stack(lat_p), jnp.stack(kr_p), jnp.stack(lat_s), jnp.stack(kr_s),
            jnp.stack(hp_l), jnp.stack(cp_l), jnp.stack(hs_l), jnp.stack(cs_l))
```
